```python
import math
import jax
import jax.numpy as jnp
from jax import lax
import numpy as np

D_MODEL = 4096
BATCH = 4
SEQ = 4096
DEPTH = 1
DEC_BATCH = 1
DEC_SEQ = 16384
PAST_LEN = 128

MEM_LEN = 256
HEAD_DIM = 128
MIX_WIDTH = D_MODEL
CONV_WIDTH = MIX_WIDTH // 2
ATTN_WIDTH = MIX_WIDTH - CONV_WIDTH
N_HEADS = ATTN_WIDTH // HEAD_DIM
N_KV_HEADS = 4
GROUP = N_HEADS // N_KV_HEADS
KV_WIDTH = N_KV_HEADS * HEAD_DIM
CONV_K = 31
WINDOW = 128
BLOCK = 128
N_BUCKETS = 32
MAX_DISTANCE = 128
X_HEADS = 4
X_WIDTH = X_HEADS * HEAD_DIM
D_FF = -(-8 * D_MODEL // (3 * 256)) * 256
IN_COLS = 2 * CONV_WIDTH + ATTN_WIDTH + 2 * KV_WIDTH
ALPHA = (2 * DEPTH) ** 0.25
BETA = (8 * DEPTH) ** -0.25
LN_EPS = 1e-5
NEG_INF = -1e30

kernel_name = "hybrid_conv_swa_memory_encoder"


def layer_norm(x, g, b):
    xf = x.astype(jnp.float32)
    mu = jnp.mean(xf, axis=-1, keepdims=True)
    var = jnp.mean(jnp.square(xf - mu), axis=-1, keepdims=True)
    return ((xf - mu) * lax.rsqrt(var + LN_EPS) * g + b).astype(x.dtype)


def t5_bucket(rel):
    nb = N_BUCKETS // 2
    max_exact = nb // 2
    ret = jnp.where(rel > 0, nb, 0)
    n = jnp.abs(rel)
    nf = jnp.maximum(n, 1).astype(jnp.float32)
    large = max_exact + (jnp.log(nf / max_exact) / math.log(MAX_DISTANCE / max_exact)
                         * (nb - max_exact)).astype(jnp.int32)
    large = jnp.minimum(large, nb - 1)
    return ret + jnp.where(n < max_exact, n, large)


def conv_group(c_val, c_gate, conv_w, conv_b, ln_g, ln_b):
    u = c_val * jax.nn.sigmoid(c_gate)
    h = lax.conv_general_dilated(
        u, conv_w[:, None, :].astype(u.dtype), window_strides=(1,),
        padding=[(CONV_K // 2, CONV_K // 2)],
        dimension_numbers=('NWC', 'WIO', 'NWC'),
        feature_group_count=CONV_WIDTH) + conv_b
    h = layer_norm(h, ln_g, ln_b)
    return jax.nn.silu(h)


def window_attention(q, k, v, sink, rel_bias):
    B, S = q.shape[0], q.shape[1]
    nb = S // BLOCK
    qb = q.reshape(B, nb, BLOCK, N_KV_HEADS, GROUP, HEAD_DIM)
    pad = ((0, 0), (BLOCK, BLOCK), (0, 0), (0, 0))
    kp = jnp.pad(k, pad)
    vp = jnp.pad(v, pad)
    idx = jnp.arange(nb)[:, None] * BLOCK + jnp.arange(3 * BLOCK)[None, :]
    kb = kp[:, idx]
    vb = vp[:, idx]
    s = jnp.einsum('bnqkgd,bnskd->bnkgqs', qb, kb,
                   preferred_element_type=jnp.float32) * (HEAD_DIM ** -0.5)
    rel = jnp.arange(3 * BLOCK)[None, :] - BLOCK - jnp.arange(BLOCK)[:, None]
    bias = rel_bias[t5_bucket(rel)].astype(jnp.float32)
    bias = jnp.transpose(bias, (2, 0, 1)).reshape(N_KV_HEADS, GROUP, BLOCK, 3 * BLOCK)
    key_pos = idx - BLOCK
    key_ok = (key_pos >= 0) & (key_pos < S)
    mask = (jnp.abs(rel) <= WINDOW)[None] & key_ok[:, None, :]
    s = jnp.where(mask[None, :, None, None], s + bias, NEG_INF)
    sk = sink.astype(jnp.float32).reshape(N_KV_HEADS, GROUP, 1, 1)
    m = jnp.maximum(jnp.max(s, axis=-1, keepdims=True), sk)
    p = jnp.exp(s - m)
    denom = jnp.sum(p, axis=-1, keepdims=True) + jnp.exp(sk - m)
    p = (p / denom).astype(v.dtype)
    o = jnp.einsum('bnkgqs,bnskd->bnqkgd', p, vb)
    return o.reshape(B, S, ATTN_WIDTH)


def memory_cross_attention(x, mem, xq_w, xk_w, xv_w, xo_w):
    B, S = x.shape[0], x.shape[1]
    q = (x @ xq_w).reshape(B, S, X_HEADS, HEAD_DIM)
    k = (mem @ xk_w).reshape(B, MEM_LEN, X_HEADS, HEAD_DIM)
    v = (mem @ xv_w).reshape(B, MEM_LEN, X_HEADS, HEAD_DIM)
    s = jnp.einsum('bqhd,bmhd->bhqm', q, k,
                   preferred_element_type=jnp.float32) * (HEAD_DIM ** -0.5)
    p = jax.nn.softmax(s, axis=-1).astype(v.dtype)
    o = jnp.einsum('bhqm,bmhd->bqhd', p, v).reshape(B, S, X_WIDTH)
    return o @ xo_w


def encoder_layer(x, mem, rel_bias, w_in, conv_w, conv_b, conv_ln_g, conv_ln_b, sink,
                  w_out, ln1_g, ln1_b, xq_w, xk_w, xv_w, xo_w, ln2_g, ln2_b,
                  w_gate, w_up, w_down, ln3_g, ln3_b):
    B, S = x.shape[0], x.shape[1]
    h = x @ w_in
    cuts = [CONV_WIDTH, 2 * CONV_WIDTH, 2 * CONV_WIDTH + ATTN_WIDTH,
            2 * CONV_WIDTH + ATTN_WIDTH + KV_WIDTH]
    c_val, c_gate, q, k, v = jnp.split(h, cuts, axis=-1)
    conv_out = conv_group(c_val, c_gate, conv_w, conv_b, conv_ln_g, conv_ln_b)
    attn_out = window_attention(q.reshape(B, S, N_HEADS, HEAD_DIM),
                                k.reshape(B, S, N_KV_HEADS, HEAD_DIM),
                                v.reshape(B, S, N_KV_HEADS, HEAD_DIM), sink, rel_bias)
    mix = jnp.concatenate([conv_out, attn_out], axis=-1) @ w_out
    x = layer_norm(ALPHA * x + mix, ln1_g, ln1_b)
    x = layer_norm(ALPHA * x + memory_cross_attention(x, mem, xq_w, xk_w, xv_w, xo_w),
                   ln2_g, ln2_b)
    f = (jax.nn.silu(x @ w_gate) * (x @ w_up)) @ w_down
    return layer_norm(ALPHA * x + f, ln3_g, ln3_b)


def run_trunk(x, mem, rel_bias, w_in, conv_w, conv_b, conv_ln_g, conv_ln_b, sink, w_out,
              ln1_g, ln1_b, xq_w, xk_w, xv_w, xo_w, ln2_g, ln2_b, w_gate, w_up, w_down,
              ln3_g, ln3_b):
    for l in range(DEPTH):
        x = encoder_layer(x, mem, rel_bias, w_in[l], conv_w[l], conv_b[l], conv_ln_g[l],
                          conv_ln_b[l], sink[l], w_out[l], ln1_g[l], ln1_b[l], xq_w[l],
                          xk_w[l], xv_w[l], xo_w[l], ln2_g[l], ln2_b[l], w_gate[l],
                          w_up[l], w_down[l], ln3_g[l], ln3_b[l])
    return x


def setup_inputs(seed: int = 0) -> dict:
    key = jax.random.key(seed)
    ks = jax.random.split(key, 32)
    L = DEPTH

    def nrm(k, shape, scale):
        return jax.random.normal(k, shape, jnp.float32) * scale

    return {
        "x_prompt": nrm(ks[0], (BATCH, SEQ, D_MODEL), 1.0),
        "x_sample": nrm(ks[1], (DEC_BATCH, DEC_SEQ, D_MODEL), 1.0),
        "mem_prompt": nrm(ks[2], (BATCH, MEM_LEN, D_MODEL), 1.0),
        "mem_sample": nrm(ks[3], (DEC_BATCH, MEM_LEN, D_MODEL), 1.0),
        "rel_bias": nrm(ks[4], (N_BUCKETS, N_HEADS), 0.5),
        "w_in": nrm(ks[5], (L, D_MODEL, IN_COLS), D_MODEL ** -0.5),
        "conv_w": nrm(ks[6], (L, CONV_K, CONV_WIDTH), CONV_K ** -0.5),
        "conv_b": nrm(ks[7], (L, CONV_WIDTH), 0.02),
        "conv_ln_g": 1.0 + nrm(ks[8], (L, CONV_WIDTH), 0.02),
        "conv_ln_b": nrm(ks[9], (L, CONV_WIDTH), 0.02),
        "sink": nrm(ks[10], (L, N_HEADS), 0.5),
        "w_out": nrm(ks[11], (L, MIX_WIDTH, D_MODEL), MIX_WIDTH ** -0.5 * BETA),
        "ln1_g": 1.0 + nrm(ks[12], (L, D_MODEL), 0.02),
        "ln1_b": nrm(ks[13], (L, D_MODEL), 0.02),
        "xq_w": nrm(ks[14], (L, D_MODEL, X_WIDTH), D_MODEL ** -0.5),
        "xk_w": nrm(ks[15], (L, D_MODEL, X_WIDTH), D_MODEL ** -0.5),
        "xv_w": nrm(ks[16], (L, D_MODEL, X_WIDTH), D_MODEL ** -0.5 * BETA),
        "xo_w": nrm(ks[17], (L, X_WIDTH, D_MODEL), X_WIDTH ** -0.5 * BETA),
        "ln2_g": 1.0 + nrm(ks[18], (L, D_MODEL), 0.02),
        "ln2_b": nrm(ks[19], (L, D_MODEL), 0.02),
        "w_gate": nrm(ks[20], (L, D_MODEL, D_FF), D_MODEL ** -0.5),
        "w_up": nrm(ks[21], (L, D_MODEL, D_FF), D_MODEL ** -0.5 * BETA),
        "w_down": nrm(ks[22], (L, D_FF, D_MODEL), D_FF ** -0.5 * BETA),
        "ln3_g": 1.0 + nrm(ks[23], (L, D_MODEL), 0.02),
        "ln3_b": nrm(ks[24], (L, D_MODEL), 0.02),
    }


def reference(x_prompt, x_sample, mem_prompt, mem_sample, rel_bias, w_in, conv_w, conv_b,
              conv_ln_g, conv_ln_b, sink, w_out, ln1_g, ln1_b, xq_w, xk_w, xv_w, xo_w,
              ln2_g, ln2_b, w_gate, w_up, w_down, ln3_g, ln3_b):
    y_prompt = run_trunk(x_prompt, mem_prompt, rel_bias, w_in, conv_w, conv_b, conv_ln_g,
                         conv_ln_b, sink, w_out, ln1_g, ln1_b, xq_w, xk_w, xv_w, xo_w,
                         ln2_g, ln2_b, w_gate, w_up, w_down, ln3_g, ln3_b)
    y_sample = run_trunk(x_sample, mem_sample, rel_bias, w_in, conv_w, conv_b, conv_ln_g,
                         conv_ln_b, sink, w_out, ln1_g, ln1_b, xq_w, xk_w, xv_w, xo_w,
                         ln2_g, ln2_b, w_gate, w_up, w_down, ln3_g, ln3_b)
    return (y_prompt, y_sample)
```

```python
import functools
import math

import jax
import jax.numpy as jnp
from jax import lax
from jax.experimental import pallas as pl
from jax.experimental.pallas import tpu as pltpu

D_MODEL = 4096
HEAD_DIM = 128
CONV_WIDTH = 2048
ATTN_WIDTH = 2048
N_HEADS = 16
N_KV_HEADS = 4
GROUP = 4
KV_WIDTH = 512
CONV_K = 31
WINDOW = 128
BLOCK = 128
N_BUCKETS = 32
MAX_DISTANCE = 128
MEM_LEN = 256
X_HEADS = 4
X_WIDTH = 512
D_FF = 11008
DEPTH = 1
ALPHA = (2 * DEPTH) ** 0.25
LN_EPS = 1e-5
NEG_INF = -1e30
SCALE = HEAD_DIM ** -0.5

V7X_VMEM_BYTES = 64 * 1024 * 1024
VMEM_LIMIT = V7X_VMEM_BYTES - 8 * 1024 * 1024
LANES = 128
CONV_HALO = 16

BF16 = jnp.bfloat16
F32 = jnp.float32


def _params(sem):
    return pltpu.CompilerParams(dimension_semantics=sem, vmem_limit_bytes=VMEM_LIMIT)


def _dot(a, b):
    return jnp.dot(a, b, preferred_element_type=F32)


def _dot_nt(a, b):
    return lax.dot_general(a, b, (((1,), (1,)), ((), ())), preferred_element_type=F32)


def _ln_rows(z, g, b):
    mu = jnp.mean(z, axis=-1, keepdims=True)
    zc = z - mu
    var = jnp.mean(zc * zc, axis=-1, keepdims=True)
    return zc * lax.rsqrt(var + LN_EPS) * g + b


def _glu_body(x_ref, wv_ref, wg_ref, o_ref):
    x = x_ref[...]
    v = _dot(x, wv_ref[...])
    g = _dot(x, wg_ref[...])
    o_ref[...] = v * jax.nn.sigmoid(g)


def _glu_call(xb, w_in):
    t = xb.shape[0]
    bm, bn = 1024, 512
    nb = CONV_WIDTH // bn
    return pl.pallas_call(
        _glu_body,
        grid=(t // bm, nb),
        in_specs=[
            pl.BlockSpec((bm, D_MODEL), lambda i, j: (i, 0)),
            pl.BlockSpec((D_MODEL, bn), lambda i, j: (0, j)),
            pl.BlockSpec((D_MODEL, bn), lambda i, j: (0, j + nb)),
        ],
        out_specs=pl.BlockSpec((bm, bn), lambda i, j: (i, j)),
        out_shape=jax.ShapeDtypeStruct((t, CONV_WIDTH), F32),
        compiler_params=_params(("parallel", "arbitrary")),
        name="glu",
    )(xb, w_in, w_in)


def _mm_body(x_ref, w_ref, o_ref):
    o_ref[...] = _dot(x_ref[...], w_ref[...]).astype(o_ref.dtype)


def _mm_call(xb, w, col_off, n, bm, bn, name):
    t, k = xb.shape
    off = col_off // bn
    return pl.pallas_call(
        _mm_body,
        grid=(t // bm, n // bn),
        in_specs=[
            pl.BlockSpec((bm, k), lambda i, j: (i, 0)),
            pl.BlockSpec((k, bn), lambda i, j: (0, j + off)),
        ],
        out_specs=pl.BlockSpec((bm, bn), lambda i, j: (i, j)),
        out_shape=jax.ShapeDtypeStruct((t, n), BF16),
        compiler_params=_params(("parallel", "arbitrary")),
        name=name,
    )(xb, w)


def _bias_body(bucket_ref, rb_ref, o_ref):
    h = pl.program_id(0)
    bkt = bucket_ref[...]
    acc = jnp.zeros(bkt.shape, F32)
    for b in range(N_BUCKETS):
        acc = jnp.where(bkt == b, rb_ref[b, h], acc)
    o_ref[0] = acc


def _t5_bucket(rel):
    nb = N_BUCKETS // 2
    max_exact = nb // 2
    ret = jnp.where(rel > 0, nb, 0)
    n = jnp.abs(rel)
    nf = jnp.maximum(n, 1).astype(F32)
    large = max_exact + (jnp.log(nf / max_exact) / math.log(MAX_DISTANCE / max_exact)
                         * (nb - max_exact)).astype(jnp.int32)
    large = jnp.minimum(large, nb - 1)
    return ret + jnp.where(n < max_exact, n, large)


def _bias_call(rel_bias):
    rel = jnp.arange(3 * BLOCK)[None, :] - BLOCK - jnp.arange(BLOCK)[:, None]
    bucket = _t5_bucket(rel).astype(jnp.int32)
    return pl.pallas_call(
        _bias_body,
        grid=(N_HEADS,),
        in_specs=[
            pl.BlockSpec((BLOCK, 3 * BLOCK), lambda h: (0, 0)),
            pl.BlockSpec(memory_space=pltpu.SMEM),
        ],
        out_specs=pl.BlockSpec((1, BLOCK, 3 * BLOCK), lambda h: (h, 0, 0)),
        out_shape=jax.ShapeDtypeStruct((N_HEADS, BLOCK, 3 * BLOCK), F32),
        compiler_params=_params(("arbitrary",)),
        name="t5bias",
    )(bucket, rel_bias)


ATTN_TQ = 512
ATTN_QB = ATTN_TQ // BLOCK


def _attn_body(seq_len, q_ref, kp_ref, km_ref, kn_ref, vp_ref, vm_ref, vn_ref, bias_ref, sink_ref,
               o_ref):
    i = pl.program_id(0)
    t0 = i * ATTN_TQ
    row = lax.broadcasted_iota(jnp.int32, (GROUP * BLOCK, 3 * BLOCK), 0) & (BLOCK - 1)
    col = lax.broadcasted_iota(jnp.int32, (GROUP * BLOCK, 3 * BLOCK), 1)
    band = (col >= row) & (col <= row + 2 * WINDOW)
    for kh in range(N_KV_HEADS):
        cs = slice(kh * HEAD_DIM, (kh + 1) * HEAD_DIM)
        kall = jnp.concatenate([kp_ref[:, cs], km_ref[:, cs], kn_ref[:, cs]], axis=0)
        vall = jnp.concatenate([vp_ref[:, cs], vm_ref[:, cs], vn_ref[:, cs]], axis=0)
        bias = jnp.concatenate([bias_ref[kh * GROUP + g] for g in range(GROUP)], axis=0)
        sk = jnp.concatenate(
            [jnp.full((BLOCK, 1), sink_ref[kh * GROUP + g], F32) for g in range(GROUP)], axis=0)
        for qb in range(ATTN_QB):
            rs = slice(qb * BLOCK, (qb + 1) * BLOCK)
            pos = t0 + qb * BLOCK
            first = lax.rem(pos, seq_len) == 0
            last = lax.rem(pos + BLOCK, seq_len) == 0
            lo = jnp.where(first, BLOCK, 0)
            hi = jnp.where(last, 2 * BLOCK, 3 * BLOCK)
            valid = band & (col >= lo) & (col < hi)
            q = jnp.concatenate(
                [q_ref[rs, (kh * GROUP + g) * HEAD_DIM:(kh * GROUP + g + 1) * HEAD_DIM]
                 for g in range(GROUP)], axis=0)
            kb = kall[qb * BLOCK:(qb + 3) * BLOCK]
            vb = vall[qb * BLOCK:(qb + 3) * BLOCK]
            s = _dot_nt(q, kb) * SCALE
            s = jnp.where(valid, s + bias, NEG_INF)
            m = jnp.maximum(jnp.max(s, axis=-1, keepdims=True), sk)
            p = jnp.exp(s - m)
            denom = jnp.sum(p, axis=-1, keepdims=True) + jnp.exp(sk - m)
            p = (p / denom).astype(BF16)
            o = _dot(p, vb).astype(BF16)
            for g in range(GROUP):
                h = kh * GROUP + g
                o_ref[rs, h * HEAD_DIM:(h + 1) * HEAD_DIM] = o[g * BLOCK:(g + 1) * BLOCK]


def _attn_call(qkv, bias, sink, seq_len):
    t = qkv.shape[0]
    nblk = t // BLOCK
    kcol = ATTN_WIDTH // KV_WIDTH
    vcol = kcol + 1

    def prev_map(c):
        return lambda i: (jnp.maximum(i * ATTN_QB - 1, 0), c)

    def next_map(c):
        return lambda i: (jnp.minimum((i + 1) * ATTN_QB, nblk - 1), c)

    return pl.pallas_call(
        functools.partial(_attn_body, seq_len),
        grid=(t // ATTN_TQ,),
        in_specs=[
            pl.BlockSpec((ATTN_TQ, ATTN_WIDTH), lambda i: (i, 0)),
            pl.BlockSpec((BLOCK, KV_WIDTH), prev_map(kcol)),
            pl.BlockSpec((ATTN_TQ, KV_WIDTH), lambda i: (i, kcol)),
            pl.BlockSpec((BLOCK, KV_WIDTH), next_map(kcol)),
            pl.BlockSpec((BLOCK, KV_WIDTH), prev_map(vcol)),
            pl.BlockSpec((ATTN_TQ, KV_WIDTH), lambda i: (i, vcol)),
            pl.BlockSpec((BLOCK, KV_WIDTH), next_map(vcol)),
            pl.BlockSpec((N_HEADS, BLOCK, 3 * BLOCK), lambda i: (0, 0, 0)),
            pl.BlockSpec(memory_space=pltpu.SMEM),
        ],
        out_specs=pl.BlockSpec((ATTN_TQ, ATTN_WIDTH), lambda i: (i, 1)),
        out_shape=jax.ShapeDtypeStruct((t, CONV_WIDTH + ATTN_WIDTH), BF16),
        compiler_params=_params(("parallel",)),
        name="attn",
    )(qkv, qkv, qkv, qkv, qkv, qkv, qkv, bias, sink)


CONV_TS = 256
CONV_RC = 32
CONV_NC = CONV_WIDTH // LANES
LN_RC = 16


def _conv_body(seq_len, up_ref, um_ref, un_ref, w_ref, cb_ref, g_ref, b_ref, mix_ref, o_ref,
               buf_ref, h_ref):
    del mix_ref
    i = pl.program_id(0)
    t0 = i * CONV_TS
    first = lax.rem(t0, seq_len) == 0
    last = lax.rem(t0 + CONV_TS, seq_len) == 0
    pscale = jnp.where(first, 0.0, 1.0).astype(F32)
    nscale = jnp.where(last, 0.0, 1.0).astype(F32)
    for c in range(CONV_NC):
        cs = slice(c * LANES, (c + 1) * LANES)
        buf_ref[c, 0:CONV_HALO, :] = up_ref[:, cs] * pscale
        buf_ref[c, CONV_HALO:CONV_HALO + CONV_TS, :] = um_ref[:, cs]
        buf_ref[c, CONV_HALO + CONV_TS:, :] = un_ref[:, cs] * nscale

    shift = CONV_HALO - CONV_K // 2

    def chunk(c, carry):
        wc = w_ref[c]
        bc = cb_ref[c]

        def rows(r, carry2):
            r0 = pl.multiple_of(r * CONV_RC, CONV_RC)
            acc = jnp.zeros((CONV_RC, LANES), F32) + bc
            for k in range(CONV_K):
                acc = acc + buf_ref[c, pl.ds(r0 + shift + k, CONV_RC), :] * wc[k:k + 1, :]
            h_ref[c, pl.ds(r0, CONV_RC), :] = acc
            return carry2

        return lax.fori_loop(0, CONV_TS // CONV_RC, rows, carry)

    lax.fori_loop(0, CONV_NC, chunk, 0)

    def norm(r, carry):
        r0 = pl.multiple_of(r * LN_RC, LN_RC)
        hb = h_ref[:, pl.ds(r0, LN_RC), :]
        mu = jnp.sum(jnp.sum(hb, axis=0), axis=-1, keepdims=True) * (1.0 / CONV_WIDTH)
        hc = hb - mu[None]
        var = jnp.sum(jnp.sum(hc * hc, axis=0), axis=-1, keepdims=True) * (1.0 / CONV_WIDTH)
        inv = lax.rsqrt(var + LN_EPS)
        for c in range(CONV_NC):
            y = hc[c] * inv * g_ref[c] + b_ref[c]
            y = y * jax.nn.sigmoid(y)
            o_ref[pl.ds(r0, LN_RC), c * LANES:(c + 1) * LANES] = y.astype(BF16)
        return carry

    lax.fori_loop(0, CONV_TS // LN_RC, norm, 0)


def _conv_call(u, conv_w, conv_b, ln_g, ln_b, mix, seq_len):
    t = u.shape[0]
    per = CONV_TS // CONV_HALO
    nh = t // CONV_HALO
    w3 = conv_w.reshape(CONV_K, CONV_NC, LANES).transpose(1, 0, 2)
    cb3 = conv_b.reshape(CONV_NC, 1, LANES)
    g3 = ln_g.reshape(CONV_NC, 1, LANES)
    b3 = ln_b.reshape(CONV_NC, 1, LANES)
    full3 = lambda a: pl.BlockSpec(a.shape, lambda i: (0, 0, 0))
    return pl.pallas_call(
        functools.partial(_conv_body, seq_len),
        grid=(t // CONV_TS,),
        in_specs=[
            pl.BlockSpec((CONV_HALO, CONV_WIDTH), lambda i: (jnp.maximum(i * per - 1, 0), 0)),
            pl.BlockSpec((CONV_TS, CONV_WIDTH), lambda i: (i, 0)),
            pl.BlockSpec((CONV_HALO, CONV_WIDTH), lambda i: (jnp.minimum((i + 1) * per, nh - 1), 0)),
            full3(w3), full3(cb3), full3(g3), full3(b3),
            pl.BlockSpec(memory_space=pl.ANY),
        ],
        out_specs=pl.BlockSpec((CONV_TS, CONV_WIDTH), lambda i: (i, 0)),
        out_shape=jax.ShapeDtypeStruct(mix.shape, mix.dtype),
        scratch_shapes=[
            pltpu.VMEM((CONV_NC, CONV_TS + 2 * CONV_HALO, LANES), F32),
            pltpu.VMEM((CONV_NC, CONV_TS, LANES), F32),
        ],
        input_output_aliases={7: 0},
        compiler_params=_params(("parallel",)),
        name="conv",
    )(u, u, u, w3, cb3, g3, b3, mix)


MIX_BM = 512
MIX_BN = 512
MIX_NJ = D_MODEL // MIX_BN
LN_ROWS = 8


def _mixout_body(m_ref, w_ref, x_ref, g_ref, b_ref, o_ref, acc_ref):
    j = pl.program_id(1)
    acc_ref[j] = ALPHA * x_ref[...] + _dot(m_ref[...], w_ref[...])

    @pl.when(j == MIX_NJ - 1)
    def _():
        def rows(r, carry):
            r0 = pl.multiple_of(r * LN_ROWS, LN_ROWS)
            z = jnp.concatenate([acc_ref[c, pl.ds(r0, LN_ROWS), :] for c in range(MIX_NJ)], axis=-1)
            o_ref[pl.ds(r0, LN_ROWS), :] = _ln_rows(z, g_ref[...], b_ref[...])
            return carry

        lax.fori_loop(0, MIX_BM // LN_ROWS, rows, 0)


def _mixout_call(mix, w_out, x, ln_g, ln_b):
    t = mix.shape[0]
    return pl.pallas_call(
        _mixout_body,
        grid=(t // MIX_BM, MIX_NJ),
        in_specs=[
            pl.BlockSpec((MIX_BM, D_MODEL), lambda i, j: (i, 0)),
            pl.BlockSpec((D_MODEL, MIX_BN), lambda i, j: (0, j)),
            pl.BlockSpec((MIX_BM, MIX_BN), lambda i, j: (i, j)),
            pl.BlockSpec((1, D_MODEL), lambda i, j: (0, 0)),
            pl.BlockSpec((1, D_MODEL), lambda i, j: (0, 0)),
        ],
        out_specs=pl.BlockSpec((MIX_BM, D_MODEL), lambda i, j: (i, 0)),
        out_shape=jax.ShapeDtypeStruct((t, D_MODEL), F32),
        scratch_shapes=[pltpu.VMEM((MIX_NJ, MIX_BM, MIX_BN), F32)],
        compiler_params=_params(("parallel", "arbitrary")),
        name="mixout",
    )(mix, w_out, x, ln_g.reshape(1, D_MODEL), ln_b.reshape(1, D_MODEL))


XA_BM = 256


def _xattn_body(x_ref, wq_ref, k_ref, v_ref, wo_ref, g_ref, b_ref, o_ref, y_ref):
    q = _dot(x_ref[...].astype(BF16), wq_ref[...])
    outs = []
    for h in range(X_HEADS):
        cs = slice(h * HEAD_DIM, (h + 1) * HEAD_DIM)
        s = _dot_nt(q[:, cs].astype(BF16), k_ref[0, :, cs]) * SCALE
        m = jnp.max(s, axis=-1, keepdims=True)
        e = jnp.exp(s - m)
        p = (e / jnp.sum(e, axis=-1, keepdims=True)).astype(BF16)
        outs.append(_dot(p, v_ref[0, :, cs]))
    o = jnp.concatenate(outs, axis=-1).astype(BF16)
    y_ref[...] = _dot(o, wo_ref[...])

    def rows(r, carry):
        r0 = pl.multiple_of(r * LN_ROWS, LN_ROWS)
        z = ALPHA * x_ref[pl.ds(r0, LN_ROWS), :] + y_ref[pl.ds(r0, LN_ROWS), :]
        o_ref[pl.ds(r0, LN_ROWS), :] = _ln_rows(z, g_ref[...], b_ref[...])
        return carry

    lax.fori_loop(0, XA_BM // LN_ROWS, rows, 0)


def _xattn_call(x1, xq_w, kx, vx, xo_w, ln_g, ln_b, seq_len):
    t = x1.shape[0]
    per = seq_len // XA_BM
    return pl.pallas_call(
        _xattn_body,
        grid=(t // XA_BM,),
        in_specs=[
            pl.BlockSpec((XA_BM, D_MODEL), lambda i: (i, 0)),
            pl.BlockSpec((D_MODEL, X_WIDTH), lambda i: (0, 0)),
            pl.BlockSpec((1, MEM_LEN, X_WIDTH), lambda i: (i // per, 0, 0)),
            pl.BlockSpec((1, MEM_LEN, X_WIDTH), lambda i: (i // per, 0, 0)),
            pl.BlockSpec((X_WIDTH, D_MODEL), lambda i: (0, 0)),
            pl.BlockSpec((1, D_MODEL), lambda i: (0, 0)),
            pl.BlockSpec((1, D_MODEL), lambda i: (0, 0)),
        ],
        out_specs=pl.BlockSpec((XA_BM, D_MODEL), lambda i: (i, 0)),
        out_shape=jax.ShapeDtypeStruct((t, D_MODEL), F32),
        scratch_shapes=[pltpu.VMEM((XA_BM, D_MODEL), F32)],
        compiler_params=_params(("parallel",)),
        name="xattn",
    )(x1, xq_w, kx, vx, xo_w, ln_g.reshape(1, D_MODEL), ln_b.reshape(1, D_MODEL))


FFN_BM = 512
FFN_TF = 256
FFN_NJ = D_FF // FFN_TF
FFN_CAST_ROWS = 16


def _ffn_body(x_ref, wg_ref, wu_ref, wd_ref, g_ref, b_ref, o_ref, xb_ref):
    j = pl.program_id(1)

    @pl.when(j == 0)
    def _():
        def rows(r, carry):
            r0 = pl.multiple_of(r * FFN_CAST_ROWS, FFN_CAST_ROWS)
            xr = x_ref[pl.ds(r0, FFN_CAST_ROWS), :]
            xb_ref[pl.ds(r0, FFN_CAST_ROWS), :] = xr.astype(BF16)
            o_ref[pl.ds(r0, FFN_CAST_ROWS), :] = ALPHA * xr
            return carry

        lax.fori_loop(0, FFN_BM // FFN_CAST_ROWS, rows, 0)

    xb = xb_ref[...]
    gate = _dot(xb, wg_ref[...])
    up = _dot(xb, wu_ref[...])
    a = (gate * jax.nn.sigmoid(gate) * up).astype(BF16)
    o_ref[...] += _dot(a, wd_ref[...])

    @pl.when(j == FFN_NJ - 1)
    def _():
        def rows(r, carry):
            r0 = pl.multiple_of(r * LN_ROWS, LN_ROWS)
            z = o_ref[pl.ds(r0, LN_ROWS), :]
            o_ref[pl.ds(r0, LN_ROWS), :] = _ln_rows(z, g_ref[...], b_ref[...])
            return carry

        lax.fori_loop(0, FFN_BM // LN_ROWS, rows, 0)


def _ffn_call(x2, w_gate, w_up, w_down, ln_g, ln_b):
    t = x2.shape[0]
    return pl.pallas_call(
        _ffn_body,
        grid=(t // FFN_BM, FFN_NJ),
        in_specs=[
            pl.BlockSpec((FFN_BM, D_MODEL), lambda i, j: (i, 0)),
            pl.BlockSpec((D_MODEL, FFN_TF), lambda i, j: (0, j)),
            pl.BlockSpec((D_MODEL, FFN_TF), lambda i, j: (0, j)),
            pl.BlockSpec((FFN_TF, D_MODEL), lambda i, j: (j, 0)),
            pl.BlockSpec((1, D_MODEL), lambda i, j: (0, 0)),
            pl.BlockSpec((1, D_MODEL), lambda i, j: (0, 0)),
        ],
        out_specs=pl.BlockSpec((FFN_BM, D_MODEL), lambda i, j: (i, 0)),
        out_shape=jax.ShapeDtypeStruct((t, D_MODEL), F32),
        scratch_shapes=[pltpu.VMEM((FFN_BM, D_MODEL), BF16)],
        compiler_params=_params(("parallel", "arbitrary")),
        name="ffn",
    )(x2, w_gate, w_up, w_down, ln_g.reshape(1, D_MODEL), ln_b.reshape(1, D_MODEL))


def _layer(x, kx, vx, bias, w, seq_len):
    xb = x.astype(BF16)
    u = _glu_call(xb, w["w_in"])
    qkv = _mm_call(xb, w["w_in"], 2 * CONV_WIDTH, ATTN_WIDTH + 2 * KV_WIDTH, 1024, 512, "qkv")
    mix = _attn_call(qkv, bias, w["sink"], seq_len)
    mix = _conv_call(u, w["conv_w"], w["conv_b"], w["conv_ln_g"], w["conv_ln_b"], mix, seq_len)
    x1 = _mixout_call(mix, w["w_out"], x, w["ln1_g"], w["ln1_b"])
    x2 = _xattn_call(x1, w["xq_w"], kx, vx, w["xo_w"], w["ln2_g"], w["ln2_b"], seq_len)
    return _ffn_call(x2, w["w_gate"], w["w_up"], w["w_down"], w["ln3_g"], w["ln3_b"])


def kernel(x_prompt, x_sample, mem_prompt, mem_sample, rel_bias, w_in, conv_w, conv_b, conv_ln_g,
           conv_ln_b, sink, w_out, ln1_g, ln1_b, xq_w, xk_w, xv_w, xo_w, ln2_g, ln2_b, w_gate, w_up,
           w_down, ln3_g, ln3_b):
    bp, sp, _ = x_prompt.shape
    bs, ss, _ = x_sample.shape
    xs = [x_prompt.reshape(bp * sp, D_MODEL), x_sample.reshape(bs * ss, D_MODEL)]
    mems = [mem_prompt, mem_sample]
    bias = _bias_call(rel_bias)
    for l in range(DEPTH):
        w = {
            "w_in": w_in[l].astype(BF16), "conv_w": conv_w[l], "conv_b": conv_b[l],
            "conv_ln_g": conv_ln_g[l], "conv_ln_b": conv_ln_b[l], "sink": sink[l],
            "w_out": w_out[l].astype(BF16), "ln1_g": ln1_g[l], "ln1_b": ln1_b[l],
            "xq_w": xq_w[l].astype(BF16), "xo_w": xo_w[l].astype(BF16),
            "ln2_g": ln2_g[l], "ln2_b": ln2_b[l],
            "w_gate": w_gate[l].astype(BF16), "w_up": w_up[l].astype(BF16),
            "w_down": w_down[l].astype(BF16), "ln3_g": ln3_g[l], "ln3_b": ln3_b[l],
        }
        wkv = jnp.concatenate([xk_w[l], xv_w[l]], axis=-1).astype(BF16)
        memb = jnp.concatenate([m.reshape(-1, D_MODEL) for m in mems], axis=0).astype(BF16)
        kv = _mm_call(memb, wkv, 0, 2 * X_WIDTH, MEM_LEN, X_WIDTH, "memkv")
        kv = kv.reshape(bp + bs, MEM_LEN, 2 * X_WIDTH)
        kx, vx = kv[..., :X_WIDTH], kv[..., X_WIDTH:]
        xs = [
            _layer(xs[0], kx[:bp], vx[:bp], bias, w, sp),
            _layer(xs[1], kx[bp:], vx[bp:], bias, w, ss),
        ]
    return (xs[0].reshape(bp, sp, D_MODEL), xs[1].reshape(bs, ss, D_MODEL))
```

```python
import functools
import math

import jax
import jax.numpy as jnp
from jax import lax
from jax.experimental import pallas as pl
from jax.experimental.pallas import tpu as pltpu

D_MODEL = 4096
HEAD_DIM = 128
CONV_WIDTH = 2048
ATTN_WIDTH = 2048
N_HEADS = 16
N_KV_HEADS = 4
GROUP = 4
KV_WIDTH = 512
CONV_K = 31
WINDOW = 128
BLOCK = 128
N_BUCKETS = 32
MAX_DISTANCE = 128
MEM_LEN = 256
X_HEADS = 4
X_WIDTH = 512
D_FF = 11008
DEPTH = 1
ALPHA = (2 * DEPTH) ** 0.25
LN_EPS = 1e-5
NEG_INF = -1e30
SCALE = HEAD_DIM ** -0.5

V7X_VMEM_BYTES = 64 * 1024 * 1024
VMEM_LIMIT = V7X_VMEM_BYTES - 8 * 1024 * 1024
LANES = 128
CONV_HALO = 16

BF16 = jnp.bfloat16
F32 = jnp.float32


def _params(sem):
    return pltpu.CompilerParams(dimension_semantics=sem, vmem_limit_bytes=VMEM_LIMIT)


def _dot(a, b):
    return jnp.dot(a, b, preferred_element_type=F32)


def _dot_nt(a, b):
    return lax.dot_general(a, b, (((1,), (1,)), ((), ())), preferred_element_type=F32)


def _ln_rows(z, g, b):
    mu = jnp.mean(z, axis=-1, keepdims=True)
    zc = z - mu
    var = jnp.mean(zc * zc, axis=-1, keepdims=True)
    return zc * lax.rsqrt(var + LN_EPS) * g + b


LN_ROWS = 64


def _ln_loop(nrows, load, store, g_ref, b_ref):
    def rows(r, carry):
        r0 = pl.multiple_of(r * LN_ROWS, LN_ROWS)
        store(r0, _ln_rows(load(r0), g_ref[...], b_ref[...]))
        return carry

    lax.fori_loop(0, nrows // LN_ROWS, rows, 0)


def _glu_body(x_ref, wv_ref, wg_ref, o_ref):
    x = x_ref[...]
    v = _dot(x, wv_ref[...])
    g = _dot(x, wg_ref[...])
    o_ref[...] = v * jax.nn.sigmoid(g)


def _glu_call(xb, w_in):
    t = xb.shape[0]
    bm, bn = 1024, 512
    nb = CONV_WIDTH // bn
    return pl.pallas_call(
        _glu_body,
        grid=(t // bm, nb),
        in_specs=[
            pl.BlockSpec((bm, D_MODEL), lambda i, j: (i, 0)),
            pl.BlockSpec((D_MODEL, bn), lambda i, j: (0, j)),
            pl.BlockSpec((D_MODEL, bn), lambda i, j: (0, j + nb)),
        ],
        out_specs=pl.BlockSpec((bm, bn), lambda i, j: (i, j)),
        out_shape=jax.ShapeDtypeStruct((t, CONV_WIDTH), F32),
        compiler_params=_params(("parallel", "arbitrary")),
        name="glu",
    )(xb, w_in, w_in)


def _mm_body(x_ref, w_ref, o_ref):
    o_ref[...] = _dot(x_ref[...], w_ref[...]).astype(o_ref.dtype)


def _mm_call(xb, w, col_off, n, bm, bn, name):
    t, k = xb.shape
    off = col_off // bn
    return pl.pallas_call(
        _mm_body,
        grid=(t // bm, n // bn),
        in_specs=[
            pl.BlockSpec((bm, k), lambda i, j: (i, 0)),
            pl.BlockSpec((k, bn), lambda i, j: (0, j + off)),
        ],
        out_specs=pl.BlockSpec((bm, bn), lambda i, j: (i, j)),
        out_shape=jax.ShapeDtypeStruct((t, n), BF16),
        compiler_params=_params(("parallel", "arbitrary")),
        name=name,
    )(xb, w)


def _bias_body(bucket_ref, rb_ref, o_ref):
    h = pl.program_id(0)
    bkt = bucket_ref[...]
    acc = jnp.zeros(bkt.shape, F32)
    for b in range(N_BUCKETS):
        acc = jnp.where(bkt == b, rb_ref[b, h], acc)
    o_ref[0] = acc


def _t5_bucket(rel):
    nb = N_BUCKETS // 2
    max_exact = nb // 2
    ret = jnp.where(rel > 0, nb, 0)
    n = jnp.abs(rel)
    nf = jnp.maximum(n, 1).astype(F32)
    large = max_exact + (jnp.log(nf / max_exact) / math.log(MAX_DISTANCE / max_exact)
                         * (nb - max_exact)).astype(jnp.int32)
    large = jnp.minimum(large, nb - 1)
    return ret + jnp.where(n < max_exact, n, large)


def _bias_call(rel_bias):
    rel = jnp.arange(3 * BLOCK)[:, None] - BLOCK - jnp.arange(BLOCK)[None, :]
    bucket = _t5_bucket(rel).astype(jnp.int32)
    return pl.pallas_call(
        _bias_body,
        grid=(N_HEADS,),
        in_specs=[
            pl.BlockSpec((3 * BLOCK, BLOCK), lambda h: (0, 0)),
            pl.BlockSpec(memory_space=pltpu.SMEM),
        ],
        out_specs=pl.BlockSpec((1, 3 * BLOCK, BLOCK), lambda h: (h, 0, 0)),
        out_shape=jax.ShapeDtypeStruct((N_HEADS, 3 * BLOCK, BLOCK), F32),
        compiler_params=_params(("arbitrary",)),
        name="t5bias",
    )(bucket, rel_bias)


ATTN_TQ = 512
ATTN_QB = ATTN_TQ // BLOCK


def _attn_body(seq_len, q_ref, kp_ref, km_ref, kn_ref, vp_ref, vm_ref, vn_ref, bias_ref, sink_ref,
               o_ref):
    i = pl.program_id(0)
    t0 = i * ATTN_TQ
    key = lax.broadcasted_iota(jnp.int32, (3 * BLOCK, GROUP * BLOCK), 0)
    qry = lax.broadcasted_iota(jnp.int32, (3 * BLOCK, GROUP * BLOCK), 1) & (BLOCK - 1)
    band = (key >= qry) & (key <= qry + 2 * WINDOW)
    for kh in range(N_KV_HEADS):
        cs = slice(kh * HEAD_DIM, (kh + 1) * HEAD_DIM)
        kall = jnp.concatenate([kp_ref[:, cs], km_ref[:, cs], kn_ref[:, cs]], axis=0)
        vt = jnp.concatenate([vp_ref[:, cs], vm_ref[:, cs], vn_ref[:, cs]], axis=0).T
        bias = jnp.concatenate([bias_ref[kh * GROUP + g] for g in range(GROUP)], axis=1)
        sk = jnp.concatenate(
            [jnp.full((1, BLOCK), sink_ref[kh * GROUP + g], F32) for g in range(GROUP)], axis=1)
        for qb in range(ATTN_QB):
            rs = slice(qb * BLOCK, (qb + 1) * BLOCK)
            pos = t0 + qb * BLOCK
            first = lax.rem(pos, seq_len) == 0
            last = lax.rem(pos + BLOCK, seq_len) == 0
            lo = jnp.where(first, BLOCK, 0)
            hi = jnp.where(last, 2 * BLOCK, 3 * BLOCK)
            valid = band & (key >= lo) & (key < hi)
            q = jnp.concatenate(
                [q_ref[rs, (kh * GROUP + g) * HEAD_DIM:(kh * GROUP + g + 1) * HEAD_DIM]
                 for g in range(GROUP)], axis=0)
            kb = kall[qb * BLOCK:(qb + 3) * BLOCK]
            s = _dot_nt(kb, q) * SCALE
            s = jnp.where(valid, s + bias, NEG_INF)
            m = jnp.maximum(jnp.max(s, axis=0, keepdims=True), sk)
            p = jnp.exp(s - m)
            denom = jnp.sum(p, axis=0, keepdims=True) + jnp.exp(sk - m)
            p = (p * (1.0 / denom)).astype(BF16)
            ot = _dot(vt[:, qb * BLOCK:(qb + 3) * BLOCK], p)
            for g in range(GROUP):
                h = kh * GROUP + g
                o_ref[rs, h * HEAD_DIM:(h + 1) * HEAD_DIM] = (
                    ot[:, g * BLOCK:(g + 1) * BLOCK].T.astype(BF16))


def _attn_call(qkv, bias, sink, seq_len):
    t = qkv.shape[0]
    nblk = t // BLOCK
    kcol = ATTN_WIDTH // KV_WIDTH
    vcol = kcol + 1

    def prev_map(c):
        return lambda i: (jnp.maximum(i * ATTN_QB - 1, 0), c)

    def next_map(c):
        return lambda i: (jnp.minimum((i + 1) * ATTN_QB, nblk - 1), c)

    return pl.pallas_call(
        functools.partial(_attn_body, seq_len),
        grid=(t // ATTN_TQ,),
        in_specs=[
            pl.BlockSpec((ATTN_TQ, ATTN_WIDTH), lambda i: (i, 0)),
            pl.BlockSpec((BLOCK, KV_WIDTH), prev_map(kcol)),
            pl.BlockSpec((ATTN_TQ, KV_WIDTH), lambda i: (i, kcol)),
            pl.BlockSpec((BLOCK, KV_WIDTH), next_map(kcol)),
            pl.BlockSpec((BLOCK, KV_WIDTH), prev_map(vcol)),
            pl.BlockSpec((ATTN_TQ, KV_WIDTH), lambda i: (i, vcol)),
            pl.BlockSpec((BLOCK, KV_WIDTH), next_map(vcol)),
            pl.BlockSpec((N_HEADS, 3 * BLOCK, BLOCK), lambda i: (0, 0, 0)),
            pl.BlockSpec(memory_space=pltpu.SMEM),
        ],
        out_specs=pl.BlockSpec((ATTN_TQ, ATTN_WIDTH), lambda i: (i, 1)),
        out_shape=jax.ShapeDtypeStruct((t, CONV_WIDTH + ATTN_WIDTH), BF16),
        compiler_params=_params(("parallel",)),
        name="attn",
    )(qkv, qkv, qkv, qkv, qkv, qkv, qkv, bias, sink)


CONV_TS = 256
CONV_RC = 32
CONV_NC = CONV_WIDTH // LANES
LN_RC = 16


def _conv_body(seq_len, up_ref, um_ref, un_ref, w_ref, cb_ref, g_ref, b_ref, mix_ref, o_ref,
               buf_ref, h_ref):
    del mix_ref
    i = pl.program_id(0)
    t0 = i * CONV_TS
    first = lax.rem(t0, seq_len) == 0
    last = lax.rem(t0 + CONV_TS, seq_len) == 0
    pscale = jnp.where(first, 0.0, 1.0).astype(F32)
    nscale = jnp.where(last, 0.0, 1.0).astype(F32)
    for c in range(CONV_NC):
        cs = slice(c * LANES, (c + 1) * LANES)
        buf_ref[c, 0:CONV_HALO, :] = up_ref[:, cs] * pscale
        buf_ref[c, CONV_HALO:CONV_HALO + CONV_TS, :] = um_ref[:, cs]
        buf_ref[c, CONV_HALO + CONV_TS:, :] = un_ref[:, cs] * nscale

    shift = CONV_HALO - CONV_K // 2

    def chunk(c, carry):
        wc = w_ref[c]
        bc = cb_ref[c]

        def rows(r, carry2):
            r0 = pl.multiple_of(r * CONV_RC, CONV_RC)
            acc = jnp.zeros((CONV_RC, LANES), F32) + bc
            for k in range(CONV_K):
                acc = acc + buf_ref[c, pl.ds(r0 + shift + k, CONV_RC), :] * wc[k:k + 1, :]
            h_ref[c, pl.ds(r0, CONV_RC), :] = acc
            return carry2

        return lax.fori_loop(0, CONV_TS // CONV_RC, rows, carry)

    lax.fori_loop(0, CONV_NC, chunk, 0)

    def norm(r, carry):
        r0 = pl.multiple_of(r * LN_RC, LN_RC)
        hb = h_ref[:, pl.ds(r0, LN_RC), :]
        mu = jnp.sum(jnp.sum(hb, axis=0), axis=-1, keepdims=True) * (1.0 / CONV_WIDTH)
        hc = hb - mu[None]
        var = jnp.sum(jnp.sum(hc * hc, axis=0), axis=-1, keepdims=True) * (1.0 / CONV_WIDTH)
        inv = lax.rsqrt(var + LN_EPS)
        for c in range(CONV_NC):
            y = hc[c] * inv * g_ref[c] + b_ref[c]
            y = y * jax.nn.sigmoid(y)
            o_ref[pl.ds(r0, LN_RC), c * LANES:(c + 1) * LANES] = y.astype(BF16)
        return carry

    lax.fori_loop(0, CONV_TS // LN_RC, norm, 0)


def _conv_call(u, conv_w, conv_b, ln_g, ln_b, mix, seq_len):
    t = u.shape[0]
    per = CONV_TS // CONV_HALO
    nh = t // CONV_HALO
    w3 = conv_w.reshape(CONV_K, CONV_NC, LANES).transpose(1, 0, 2)
    cb3 = conv_b.reshape(CONV_NC, 1, LANES)
    g3 = ln_g.reshape(CONV_NC, 1, LANES)
    b3 = ln_b.reshape(CONV_NC, 1, LANES)
    full3 = lambda a: pl.BlockSpec(a.shape, lambda i: (0, 0, 0))
    return pl.pallas_call(
        functools.partial(_conv_body, seq_len),
        grid=(t // CONV_TS,),
        in_specs=[
            pl.BlockSpec((CONV_HALO, CONV_WIDTH), lambda i: (jnp.maximum(i * per - 1, 0), 0)),
            pl.BlockSpec((CONV_TS, CONV_WIDTH), lambda i: (i, 0)),
            pl.BlockSpec((CONV_HALO, CONV_WIDTH), lambda i: (jnp.minimum((i + 1) * per, nh - 1), 0)),
            full3(w3), full3(cb3), full3(g3), full3(b3),
            pl.BlockSpec(memory_space=pl.ANY),
        ],
        out_specs=pl.BlockSpec((CONV_TS, CONV_WIDTH), lambda i: (i, 0)),
        out_shape=jax.ShapeDtypeStruct(mix.shape, mix.dtype),
        scratch_shapes=[
            pltpu.VMEM((CONV_NC, CONV_TS + 2 * CONV_HALO, LANES), F32),
            pltpu.VMEM((CONV_NC, CONV_TS, LANES), F32),
        ],
        input_output_aliases={7: 0},
        compiler_params=_params(("parallel",)),
        name="conv",
    )(u, u, u, w3, cb3, g3, b3, mix)


MIX_BM = 1024
MIX_BN = 512


def _mixout_body(m_ref, w_ref, x_ref, o_ref):
    o_ref[...] = ALPHA * x_ref[...] + _dot(m_ref[...], w_ref[...])


def _mixout_call(mix, w_out, x):
    t = mix.shape[0]
    return pl.pallas_call(
        _mixout_body,
        grid=(t // MIX_BM, D_MODEL // MIX_BN),
        in_specs=[
            pl.BlockSpec((MIX_BM, D_MODEL), lambda i, j: (i, 0)),
            pl.BlockSpec((D_MODEL, MIX_BN), lambda i, j: (0, j)),
            pl.BlockSpec((MIX_BM, MIX_BN), lambda i, j: (i, j)),
        ],
        out_specs=pl.BlockSpec((MIX_BM, MIX_BN), lambda i, j: (i, j)),
        out_shape=jax.ShapeDtypeStruct((t, D_MODEL), F32),
        compiler_params=_params(("parallel", "arbitrary")),
        name="mixout",
    )(mix, w_out, x)


XA_BM = 256


def _xattn_body(z_ref, g1_ref, b1_ref, wq_ref, k_ref, v_ref, wo_ref, g2_ref, b2_ref, o_ref,
                x1_ref, x1b_ref, y_ref):
    def store_x1(r0, x1):
        x1_ref[pl.ds(r0, LN_ROWS), :] = x1
        x1b_ref[pl.ds(r0, LN_ROWS), :] = x1.astype(BF16)

    _ln_loop(XA_BM, lambda r0: z_ref[pl.ds(r0, LN_ROWS), :], store_x1, g1_ref, b1_ref)

    q = _dot(x1b_ref[...], wq_ref[...])
    outs = []
    for h in range(X_HEADS):
        cs = slice(h * HEAD_DIM, (h + 1) * HEAD_DIM)
        s = _dot_nt(q[:, cs].astype(BF16), k_ref[0, :, cs]) * SCALE
        m = jnp.max(s, axis=-1, keepdims=True)
        e = jnp.exp(s - m)
        p = (e / jnp.sum(e, axis=-1, keepdims=True)).astype(BF16)
        outs.append(_dot(p, v_ref[0, :, cs]))
    o = jnp.concatenate(outs, axis=-1).astype(BF16)
    y_ref[...] = _dot(o, wo_ref[...])

    def load_z2(r0):
        return ALPHA * x1_ref[pl.ds(r0, LN_ROWS), :] + y_ref[pl.ds(r0, LN_ROWS), :]

    def store_x2(r0, x2):
        o_ref[pl.ds(r0, LN_ROWS), :] = x2

    _ln_loop(XA_BM, load_z2, store_x2, g2_ref, b2_ref)


def _xattn_call(z1, ln1_g, ln1_b, xq_w, kx, vx, xo_w, ln2_g, ln2_b, seq_len):
    t = z1.shape[0]
    per = seq_len // XA_BM
    vec = pl.BlockSpec((1, D_MODEL), lambda i: (0, 0))
    return pl.pallas_call(
        _xattn_body,
        grid=(t // XA_BM,),
        in_specs=[
            pl.BlockSpec((XA_BM, D_MODEL), lambda i: (i, 0)),
            vec, vec,
            pl.BlockSpec((D_MODEL, X_WIDTH), lambda i: (0, 0)),
            pl.BlockSpec((1, MEM_LEN, X_WIDTH), lambda i: (i // per, 0, 0)),
            pl.BlockSpec((1, MEM_LEN, X_WIDTH), lambda i: (i // per, 0, 0)),
            pl.BlockSpec((X_WIDTH, D_MODEL), lambda i: (0, 0)),
            vec, vec,
        ],
        out_specs=pl.BlockSpec((XA_BM, D_MODEL), lambda i: (i, 0)),
        out_shape=jax.ShapeDtypeStruct((t, D_MODEL), F32),
        scratch_shapes=[
            pltpu.VMEM((XA_BM, D_MODEL), F32),
            pltpu.VMEM((XA_BM, D_MODEL), BF16),
            pltpu.VMEM((XA_BM, D_MODEL), F32),
        ],
        compiler_params=_params(("parallel",)),
        name="xattn",
    )(z1, ln1_g.reshape(1, D_MODEL), ln1_b.reshape(1, D_MODEL), xq_w, kx, vx, xo_w,
      ln2_g.reshape(1, D_MODEL), ln2_b.reshape(1, D_MODEL))


FFN_BM = 512
FFN_TF = 256
FFN_NJ = D_FF // FFN_TF
FFN_CAST_ROWS = 16


def _ffn_body(x_ref, wg_ref, wu_ref, wd_ref, g_ref, b_ref, o_ref, xb_ref):
    j = pl.program_id(1)

    @pl.when(j == 0)
    def _():
        def rows(r, carry):
            r0 = pl.multiple_of(r * FFN_CAST_ROWS, FFN_CAST_ROWS)
            xr = x_ref[pl.ds(r0, FFN_CAST_ROWS), :]
            xb_ref[pl.ds(r0, FFN_CAST_ROWS), :] = xr.astype(BF16)
            o_ref[pl.ds(r0, FFN_CAST_ROWS), :] = ALPHA * xr
            return carry

        lax.fori_loop(0, FFN_BM // FFN_CAST_ROWS, rows, 0)

    xb = xb_ref[...]
    gate = _dot(xb, wg_ref[...])
    up = _dot(xb, wu_ref[...])
    a = (gate * jax.nn.sigmoid(gate) * up).astype(BF16)
    o_ref[...] += _dot(a, wd_ref[...])

    @pl.when(j == FFN_NJ - 1)
    def _():
        def store(r0, y):
            o_ref[pl.ds(r0, LN_ROWS), :] = y

        _ln_loop(FFN_BM, lambda r0: o_ref[pl.ds(r0, LN_ROWS), :], store, g_ref, b_ref)


def _ffn_call(x2, w_gate, w_up, w_down, ln_g, ln_b):
    t = x2.shape[0]
    return pl.pallas_call(
        _ffn_body,
        grid=(t // FFN_BM, FFN_NJ),
        in_specs=[
            pl.BlockSpec((FFN_BM, D_MODEL), lambda i, j: (i, 0)),
            pl.BlockSpec((D_MODEL, FFN_TF), lambda i, j: (0, j)),
            pl.BlockSpec((D_MODEL, FFN_TF), lambda i, j: (0, j)),
            pl.BlockSpec((FFN_TF, D_MODEL), lambda i, j: (j, 0)),
            pl.BlockSpec((1, D_MODEL), lambda i, j: (0, 0)),
            pl.BlockSpec((1, D_MODEL), lambda i, j: (0, 0)),
        ],
        out_specs=pl.BlockSpec((FFN_BM, D_MODEL), lambda i, j: (i, 0)),
        out_shape=jax.ShapeDtypeStruct((t, D_MODEL), F32),
        scratch_shapes=[pltpu.VMEM((FFN_BM, D_MODEL), BF16)],
        compiler_params=_params(("parallel", "arbitrary")),
        name="ffn",
    )(x2, w_gate, w_up, w_down, ln_g.reshape(1, D_MODEL), ln_b.reshape(1, D_MODEL))


def _layer(x, kx, vx, bias, w, seq_len):
    xb = x.astype(BF16)
    u = _glu_call(xb, w["w_in"])
    qkv = _mm_call(xb, w["w_in"], 2 * CONV_WIDTH, ATTN_WIDTH + 2 * KV_WIDTH, 1024, 512, "qkv")
    mix = _attn_call(qkv, bias, w["sink"], seq_len)
    mix = _conv_call(u, w["conv_w"], w["conv_b"], w["conv_ln_g"], w["conv_ln_b"], mix, seq_len)
    z1 = _mixout_call(mix, w["w_out"], x)
    x2 = _xattn_call(z1, w["ln1_g"], w["ln1_b"], w["xq_w"], kx, vx, w["xo_w"], w["ln2_g"], w["ln2_b"],
                     seq_len)
    return _ffn_call(x2, w["w_gate"], w["w_up"], w["w_down"], w["ln3_g"], w["ln3_b"])


def kernel(x_prompt, x_sample, mem_prompt, mem_sample, rel_bias, w_in, conv_w, conv_b, conv_ln_g,
           conv_ln_b, sink, w_out, ln1_g, ln1_b, xq_w, xk_w, xv_w, xo_w, ln2_g, ln2_b, w_gate, w_up,
           w_down, ln3_g, ln3_b):
    bp, sp, _ = x_prompt.shape
    bs, ss, _ = x_sample.shape
    xs = [x_prompt.reshape(bp * sp, D_MODEL), x_sample.reshape(bs * ss, D_MODEL)]
    mems = [mem_prompt, mem_sample]
    bias = _bias_call(rel_bias)
    for l in range(DEPTH):
        w = {
            "w_in": w_in[l].astype(BF16), "conv_w": conv_w[l], "conv_b": conv_b[l],
            "conv_ln_g": conv_ln_g[l], "conv_ln_b": conv_ln_b[l], "sink": sink[l],
            "w_out": w_out[l].astype(BF16), "ln1_g": ln1_g[l], "ln1_b": ln1_b[l],
            "xq_w": xq_w[l].astype(BF16), "xo_w": xo_w[l].astype(BF16),
            "ln2_g": ln2_g[l], "ln2_b": ln2_b[l],
            "w_gate": w_gate[l].astype(BF16), "w_up": w_up[l].astype(BF16),
            "w_down": w_down[l].astype(BF16), "ln3_g": ln3_g[l], "ln3_b": ln3_b[l],
        }
        wkv = jnp.concatenate([xk_w[l], xv_w[l]], axis=-1).astype(BF16)
        memb = jnp.concatenate([m.reshape(-1, D_MODEL) for m in mems], axis=0).astype(BF16)
        kv = _mm_call(memb, wkv, 0, 2 * X_WIDTH, MEM_LEN, X_WIDTH, "memkv")
        kv = kv.reshape(bp + bs, MEM_LEN, 2 * X_WIDTH)
        kx, vx = kv[..., :X_WIDTH], kv[..., X_WIDTH:]
        xs = [
            _layer(xs[0], kx[:bp], vx[:bp], bias, w, sp),
            _layer(xs[1], kx[bp:], vx[bp:], bias, w, ss),
        ]
    return (xs[0].reshape(bp, sp, D_MODEL), xs[1].reshape(bs, ss, D_MODEL))
```

```python
import functools
import math

import jax
import jax.numpy as jnp
from jax import lax
from jax.experimental import pallas as pl
from jax.experimental.pallas import tpu as pltpu

D_MODEL = 4096
HEAD_DIM = 128
CONV_WIDTH = 2048
ATTN_WIDTH = 2048
N_HEADS = 16
N_KV_HEADS = 4
GROUP = 4
KV_WIDTH = 512
CONV_K = 31
WINDOW = 128
BLOCK = 128
N_BUCKETS = 32
MAX_DISTANCE = 128
MEM_LEN = 256
X_HEADS = 4
X_WIDTH = 512
D_FF = 11008
DEPTH = 1
ALPHA = (2 * DEPTH) ** 0.25
LN_EPS = 1e-5
NEG_INF = -1e30
SCALE = HEAD_DIM ** -0.5

V7X_VMEM_BYTES = 64 * 1024 * 1024
VMEM_LIMIT = V7X_VMEM_BYTES - 8 * 1024 * 1024
LANES = 128
CONV_HALO = 16

BF16 = jnp.bfloat16
F32 = jnp.float32


def _params(sem):
    return pltpu.CompilerParams(dimension_semantics=sem, vmem_limit_bytes=VMEM_LIMIT)


def _dot(a, b):
    return jnp.dot(a, b, preferred_element_type=F32)


def _dot_nt(a, b):
    return lax.dot_general(a, b, (((1,), (1,)), ((), ())), preferred_element_type=F32)


LN_ROWS = 128


def _ln_loop(nrows, load, store, g_ref, b_ref):
    def rows(r, carry):
        r0 = pl.multiple_of(r * LN_ROWS, LN_ROWS)
        z = load(r0)
        mu = jnp.mean(z, axis=-1, keepdims=True)
        zc = z - mu
        var = jnp.mean(zc * zc, axis=-1, keepdims=True)
        store(r0, zc * lax.rsqrt(var + LN_EPS) * g_ref[...] + b_ref[...])
        return carry

    lax.fori_loop(0, nrows // LN_ROWS, rows, 0)


def _glu_body(x_ref, wv_ref, wg_ref, o_ref):
    x = x_ref[...]
    v = _dot(x, wv_ref[...])
    g = _dot(x, wg_ref[...])
    o_ref[...] = v * jax.nn.sigmoid(g)


def _glu_call(xb, w_in):
    t = xb.shape[0]
    bm, bn = 1024, 512
    nb = CONV_WIDTH // bn
    return pl.pallas_call(
        _glu_body,
        grid=(t // bm, nb),
        in_specs=[
            pl.BlockSpec((bm, D_MODEL), lambda i, j: (i, 0)),
            pl.BlockSpec((D_MODEL, bn), lambda i, j: (0, j)),
            pl.BlockSpec((D_MODEL, bn), lambda i, j: (0, j + nb)),
        ],
        out_specs=pl.BlockSpec((bm, bn), lambda i, j: (i, j)),
        out_shape=jax.ShapeDtypeStruct((t, CONV_WIDTH), F32),
        compiler_params=_params(("parallel", "arbitrary")),
        name="glu",
    )(xb, w_in, w_in)


def _mm_body(x_ref, w_ref, o_ref):
    o_ref[...] = _dot(x_ref[...], w_ref[...]).astype(o_ref.dtype)


def _mm_call(xb, w, col_off, n, bm, bn, name):
    t, k = xb.shape
    off = col_off // bn
    return pl.pallas_call(
        _mm_body,
        grid=(t // bm, n // bn),
        in_specs=[
            pl.BlockSpec((bm, k), lambda i, j: (i, 0)),
            pl.BlockSpec((k, bn), lambda i, j: (0, j + off)),
        ],
        out_specs=pl.BlockSpec((bm, bn), lambda i, j: (i, j)),
        out_shape=jax.ShapeDtypeStruct((t, n), BF16),
        compiler_params=_params(("parallel", "arbitrary")),
        name=name,
    )(xb, w)


def _bias_body(bucket_ref, rb_ref, o_ref):
    h = pl.program_id(0)
    bkt = bucket_ref[...]
    acc = jnp.zeros(bkt.shape, F32)
    for b in range(N_BUCKETS):
        acc = jnp.where(bkt == b, rb_ref[b, h], acc)
    o_ref[0] = acc


def _t5_bucket(rel):
    nb = N_BUCKETS // 2
    max_exact = nb // 2
    ret = jnp.where(rel > 0, nb, 0)
    n = jnp.abs(rel)
    nf = jnp.maximum(n, 1).astype(F32)
    large = max_exact + (jnp.log(nf / max_exact) / math.log(MAX_DISTANCE / max_exact)
                         * (nb - max_exact)).astype(jnp.int32)
    large = jnp.minimum(large, nb - 1)
    return ret + jnp.where(n < max_exact, n, large)


def _bias_call(rel_bias):
    rel = jnp.arange(3 * BLOCK)[:, None] - BLOCK - jnp.arange(BLOCK)[None, :]
    bucket = _t5_bucket(rel).astype(jnp.int32)
    return pl.pallas_call(
        _bias_body,
        grid=(N_HEADS,),
        in_specs=[
            pl.BlockSpec((3 * BLOCK, BLOCK), lambda h: (0, 0)),
            pl.BlockSpec(memory_space=pltpu.SMEM),
        ],
        out_specs=pl.BlockSpec((1, 3 * BLOCK, BLOCK), lambda h: (h, 0, 0)),
        out_shape=jax.ShapeDtypeStruct((N_HEADS, 3 * BLOCK, BLOCK), F32),
        compiler_params=_params(("arbitrary",)),
        name="t5bias",
    )(bucket, rel_bias)


ATTN_TQ = 512
ATTN_QB = ATTN_TQ // BLOCK


def _attn_body(seq_len, q_ref, kp_ref, km_ref, kn_ref, vp_ref, vm_ref, vn_ref, bias_ref, sink_ref,
               o_ref):
    i = pl.program_id(0)
    t0 = i * ATTN_TQ
    key = lax.broadcasted_iota(jnp.int32, (3 * BLOCK, GROUP * BLOCK), 0)
    qry = lax.broadcasted_iota(jnp.int32, (3 * BLOCK, GROUP * BLOCK), 1) & (BLOCK - 1)
    band = (key >= qry) & (key <= qry + 2 * WINDOW)
    for kh in range(N_KV_HEADS):
        cs = slice(kh * HEAD_DIM, (kh + 1) * HEAD_DIM)
        kall = jnp.concatenate([kp_ref[:, cs], km_ref[:, cs], kn_ref[:, cs]], axis=0)
        vt = jnp.concatenate([vp_ref[:, cs], vm_ref[:, cs], vn_ref[:, cs]], axis=0).T
        bias = jnp.concatenate([bias_ref[kh * GROUP + g] for g in range(GROUP)], axis=1)
        sk = jnp.concatenate(
            [jnp.full((1, BLOCK), sink_ref[kh * GROUP + g], F32) for g in range(GROUP)], axis=1)
        for qb in range(ATTN_QB):
            rs = slice(qb * BLOCK, (qb + 1) * BLOCK)
            pos = t0 + qb * BLOCK
            first = lax.rem(pos, seq_len) == 0
            last = lax.rem(pos + BLOCK, seq_len) == 0
            lo = jnp.where(first, BLOCK, 0)
            hi = jnp.where(last, 2 * BLOCK, 3 * BLOCK)
            valid = band & (key >= lo) & (key < hi)
            q = jnp.concatenate(
                [q_ref[rs, (kh * GROUP + g) * HEAD_DIM:(kh * GROUP + g + 1) * HEAD_DIM]
                 for g in range(GROUP)], axis=0)
            kb = kall[qb * BLOCK:(qb + 3) * BLOCK]
            s = _dot_nt(kb, q) * SCALE
            s = jnp.where(valid, s + bias, NEG_INF)
            m = jnp.maximum(jnp.max(s, axis=0, keepdims=True), sk)
            p = jnp.exp(s - m)
            denom = jnp.sum(p, axis=0, keepdims=True) + jnp.exp(sk - m)
            p = (p * (1.0 / denom)).astype(BF16)
            ot = _dot(vt[:, qb * BLOCK:(qb + 3) * BLOCK], p)
            for g in range(GROUP):
                h = kh * GROUP + g
                o_ref[rs, h * HEAD_DIM:(h + 1) * HEAD_DIM] = (
                    ot[:, g * BLOCK:(g + 1) * BLOCK].T.astype(BF16))


def _attn_call(qkv, bias, sink, seq_len):
    t = qkv.shape[0]
    nblk = t // BLOCK
    kcol = ATTN_WIDTH // KV_WIDTH
    vcol = kcol + 1

    def prev_map(c):
        return lambda i: (jnp.maximum(i * ATTN_QB - 1, 0), c)

    def next_map(c):
        return lambda i: (jnp.minimum((i + 1) * ATTN_QB, nblk - 1), c)

    return pl.pallas_call(
        functools.partial(_attn_body, seq_len),
        grid=(t // ATTN_TQ,),
        in_specs=[
            pl.BlockSpec((ATTN_TQ, ATTN_WIDTH), lambda i: (i, 0)),
            pl.BlockSpec((BLOCK, KV_WIDTH), prev_map(kcol)),
            pl.BlockSpec((ATTN_TQ, KV_WIDTH), lambda i: (i, kcol)),
            pl.BlockSpec((BLOCK, KV_WIDTH), next_map(kcol)),
            pl.BlockSpec((BLOCK, KV_WIDTH), prev_map(vcol)),
            pl.BlockSpec((ATTN_TQ, KV_WIDTH), lambda i: (i, vcol)),
            pl.BlockSpec((BLOCK, KV_WIDTH), next_map(vcol)),
            pl.BlockSpec((N_HEADS, 3 * BLOCK, BLOCK), lambda i: (0, 0, 0)),
            pl.BlockSpec(memory_space=pltpu.SMEM),
        ],
        out_specs=pl.BlockSpec((ATTN_TQ, ATTN_WIDTH), lambda i: (i, 1)),
        out_shape=jax.ShapeDtypeStruct((t, CONV_WIDTH + ATTN_WIDTH), BF16),
        compiler_params=_params(("parallel",)),
        name="attn",
    )(qkv, qkv, qkv, qkv, qkv, qkv, qkv, bias, sink)


CONV_TS = 256
CONV_RC = 64
CONV_NC = CONV_WIDTH // LANES
SUBLANES = 8


def _conv_body(seq_len, up_ref, um_ref, un_ref, w_ref, cb_ref, g_ref, b_ref, mix_ref, o_ref,
               buf_ref, h_ref):
    del mix_ref
    i = pl.program_id(0)
    t0 = i * CONV_TS
    first = lax.rem(t0, seq_len) == 0
    last = lax.rem(t0 + CONV_TS, seq_len) == 0
    shift = CONV_HALO - CONV_K // 2
    groups = CONV_RC // SUBLANES

    for c in range(CONV_NC):
        cs = slice(c * LANES, (c + 1) * LANES)
        buf_ref[c, 0:CONV_HALO, :] = jnp.where(first, 0.0, up_ref[:, cs])
        buf_ref[c, CONV_HALO:CONV_HALO + CONV_TS, :] = um_ref[:, cs]
        buf_ref[c, CONV_HALO + CONV_TS:, :] = jnp.where(last, 0.0, un_ref[:, cs])

        def rows(r, carry, c=c, cs=cs):
            r0 = pl.multiple_of(r * CONV_RC, CONV_RC)
            win = [buf_ref[c, pl.ds(r0 + shift + off, SUBLANES), :]
                   for off in range(CONV_K + CONV_RC - SUBLANES)]
            accs = [cb_ref[:, cs]] * groups
            for k in range(CONV_K):
                wk = w_ref[k * SUBLANES:(k + 1) * SUBLANES, cs]
                accs = [accs[j] + win[k + SUBLANES * j] * wk for j in range(groups)]
            for j in range(groups):
                h_ref[pl.ds(r0 + SUBLANES * j, SUBLANES), cs] = accs[j]
            return carry

        lax.fori_loop(0, CONV_TS // CONV_RC, rows, 0)

    def store(r0, y):
        o_ref[pl.ds(r0, LN_ROWS), :] = (y * jax.nn.sigmoid(y)).astype(BF16)

    _ln_loop(CONV_TS, lambda r0: h_ref[pl.ds(r0, LN_ROWS), :], store, g_ref, b_ref)


def _conv_call(u, conv_w, conv_b, ln_g, ln_b, mix, seq_len):
    t = u.shape[0]
    per = CONV_TS // CONV_HALO
    nh = t // CONV_HALO
    w_rep = jnp.repeat(conv_w, SUBLANES, axis=0)
    cb_rep = jnp.broadcast_to(conv_b[None, :], (SUBLANES, CONV_WIDTH))
    full = lambda a: pl.BlockSpec(a.shape, lambda i: (0, 0))
    g2, b2 = ln_g.reshape(1, CONV_WIDTH), ln_b.reshape(1, CONV_WIDTH)
    return pl.pallas_call(
        functools.partial(_conv_body, seq_len),
        grid=(t // CONV_TS,),
        in_specs=[
            pl.BlockSpec((CONV_HALO, CONV_WIDTH), lambda i: (jnp.maximum(i * per - 1, 0), 0)),
            pl.BlockSpec((CONV_TS, CONV_WIDTH), lambda i: (i, 0)),
            pl.BlockSpec((CONV_HALO, CONV_WIDTH), lambda i: (jnp.minimum((i + 1) * per, nh - 1), 0)),
            full(w_rep), full(cb_rep), full(g2), full(b2),
            pl.BlockSpec(memory_space=pl.ANY),
        ],
        out_specs=pl.BlockSpec((CONV_TS, CONV_WIDTH), lambda i: (i, 0)),
        out_shape=jax.ShapeDtypeStruct(mix.shape, mix.dtype),
        scratch_shapes=[
            pltpu.VMEM((CONV_NC, CONV_TS + 2 * CONV_HALO, LANES), F32),
            pltpu.VMEM((CONV_TS, CONV_WIDTH), F32),
        ],
        input_output_aliases={7: 0},
        compiler_params=_params(("parallel",)),
        name="conv",
    )(u, u, u, w_rep, cb_rep, g2, b2, mix)


MIX_BM = 1024
MIX_BN = 512


def _mixout_body(m_ref, w_ref, x_ref, o_ref):
    o_ref[...] = ALPHA * x_ref[...] + _dot(m_ref[...], w_ref[...])


def _mixout_call(mix, w_out, x):
    t = mix.shape[0]
    return pl.pallas_call(
        _mixout_body,
        grid=(t // MIX_BM, D_MODEL // MIX_BN),
        in_specs=[
            pl.BlockSpec((MIX_BM, D_MODEL), lambda i, j: (i, 0)),
            pl.BlockSpec((D_MODEL, MIX_BN), lambda i, j: (0, j)),
            pl.BlockSpec((MIX_BM, MIX_BN), lambda i, j: (i, j)),
        ],
        out_specs=pl.BlockSpec((MIX_BM, MIX_BN), lambda i, j: (i, j)),
        out_shape=jax.ShapeDtypeStruct((t, D_MODEL), F32),
        compiler_params=_params(("parallel", "arbitrary")),
        name="mixout",
    )(mix, w_out, x)


XA_BM = 256


def _xattn_body(z_ref, g1_ref, b1_ref, wq_ref, k_ref, v_ref, wo_ref, g2_ref, b2_ref, o_ref,
                x1_ref, x1b_ref, z2_ref):
    def store_x1(r0, x1):
        x1_ref[pl.ds(r0, LN_ROWS), :] = x1
        x1b_ref[pl.ds(r0, LN_ROWS), :] = x1.astype(BF16)

    _ln_loop(XA_BM, lambda r0: z_ref[pl.ds(r0, LN_ROWS), :], store_x1, g1_ref, b1_ref)

    q = _dot(x1b_ref[...], wq_ref[...])
    outs = []
    for h in range(X_HEADS):
        cs = slice(h * HEAD_DIM, (h + 1) * HEAD_DIM)
        s = _dot_nt(q[:, cs].astype(BF16), k_ref[0, :, cs]) * SCALE
        m = jnp.max(s, axis=-1, keepdims=True)
        e = jnp.exp(s - m)
        p = (e / jnp.sum(e, axis=-1, keepdims=True)).astype(BF16)
        outs.append(_dot(p, v_ref[0, :, cs]))
    o = jnp.concatenate(outs, axis=-1).astype(BF16)
    z2_ref[...] = ALPHA * x1_ref[...] + _dot(o, wo_ref[...])

    def store_x2(r0, x2):
        o_ref[pl.ds(r0, LN_ROWS), :] = x2

    _ln_loop(XA_BM, lambda r0: z2_ref[pl.ds(r0, LN_ROWS), :], store_x2, g2_ref, b2_ref)


def _xattn_call(z1, ln1_g, ln1_b, xq_w, kx, vx, xo_w, ln2_g, ln2_b, seq_len):
    t = z1.shape[0]
    per = seq_len // XA_BM
    vec = pl.BlockSpec((1, D_MODEL), lambda i: (0, 0))
    return pl.pallas_call(
        _xattn_body,
        grid=(t // XA_BM,),
        in_specs=[
            pl.BlockSpec((XA_BM, D_MODEL), lambda i: (i, 0)),
            vec, vec,
            pl.BlockSpec((D_MODEL, X_WIDTH), lambda i: (0, 0)),
            pl.BlockSpec((1, MEM_LEN, X_WIDTH), lambda i: (i // per, 0, 0)),
            pl.BlockSpec((1, MEM_LEN, X_WIDTH), lambda i: (i // per, 0, 0)),
            pl.BlockSpec((X_WIDTH, D_MODEL), lambda i: (0, 0)),
            vec, vec,
        ],
        out_specs=pl.BlockSpec((XA_BM, D_MODEL), lambda i: (i, 0)),
        out_shape=jax.ShapeDtypeStruct((t, D_MODEL), F32),
        scratch_shapes=[
            pltpu.VMEM((XA_BM, D_MODEL), F32),
            pltpu.VMEM((XA_BM, D_MODEL), BF16),
            pltpu.VMEM((XA_BM, D_MODEL), F32),
        ],
        compiler_params=_params(("parallel",)),
        name="xattn",
    )(z1, ln1_g.reshape(1, D_MODEL), ln1_b.reshape(1, D_MODEL), xq_w, kx, vx, xo_w,
      ln2_g.reshape(1, D_MODEL), ln2_b.reshape(1, D_MODEL))


FFN_BM = 512
FFN_TF = 256
FFN_NJ = D_FF // FFN_TF
FFN_CAST_ROWS = 16


def _ffn_body(x_ref, wg_ref, wu_ref, wd_ref, g_ref, b_ref, o_ref, xb_ref):
    j = pl.program_id(1)

    @pl.when(j == 0)
    def _():
        def rows(r, carry):
            r0 = pl.multiple_of(r * FFN_CAST_ROWS, FFN_CAST_ROWS)
            xr = x_ref[pl.ds(r0, FFN_CAST_ROWS), :]
            xb_ref[pl.ds(r0, FFN_CAST_ROWS), :] = xr.astype(BF16)
            o_ref[pl.ds(r0, FFN_CAST_ROWS), :] = ALPHA * xr
            return carry

        lax.fori_loop(0, FFN_BM // FFN_CAST_ROWS, rows, 0)

    xb = xb_ref[...]
    gate = _dot(xb, wg_ref[...])
    up = _dot(xb, wu_ref[...])
    a = (gate * jax.nn.sigmoid(gate) * up).astype(BF16)
    o_ref[...] += _dot(a, wd_ref[...])

    @pl.when(j == FFN_NJ - 1)
    def _():
        def store(r0, y):
            o_ref[pl.ds(r0, LN_ROWS), :] = y

        _ln_loop(FFN_BM, lambda r0: o_ref[pl.ds(r0, LN_ROWS), :], store, g_ref, b_ref)


def _ffn_call(x2, w_gate, w_up, w_down, ln_g, ln_b):
    t = x2.shape[0]
    return pl.pallas_call(
        _ffn_body,
        grid=(t // FFN_BM, FFN_NJ),
        in_specs=[
            pl.BlockSpec((FFN_BM, D_MODEL), lambda i, j: (i, 0)),
            pl.BlockSpec((D_MODEL, FFN_TF), lambda i, j: (0, j)),
            pl.BlockSpec((D_MODEL, FFN_TF), lambda i, j: (0, j)),
            pl.BlockSpec((FFN_TF, D_MODEL), lambda i, j: (j, 0)),
            pl.BlockSpec((1, D_MODEL), lambda i, j: (0, 0)),
            pl.BlockSpec((1, D_MODEL), lambda i, j: (0, 0)),
        ],
        out_specs=pl.BlockSpec((FFN_BM, D_MODEL), lambda i, j: (i, 0)),
        out_shape=jax.ShapeDtypeStruct((t, D_MODEL), F32),
        scratch_shapes=[pltpu.VMEM((FFN_BM, D_MODEL), BF16)],
        compiler_params=_params(("parallel", "arbitrary")),
        name="ffn",
    )(x2, w_gate, w_up, w_down, ln_g.reshape(1, D_MODEL), ln_b.reshape(1, D_MODEL))


def _layer(x, kx, vx, bias, w, seq_len):
    xb = x.astype(BF16)
    u = _glu_call(xb, w["w_in"])
    qkv = _mm_call(xb, w["w_in"], 2 * CONV_WIDTH, ATTN_WIDTH + 2 * KV_WIDTH, 1024, 512, "qkv")
    mix = _attn_call(qkv, bias, w["sink"], seq_len)
    mix = _conv_call(u, w["conv_w"], w["conv_b"], w["conv_ln_g"], w["conv_ln_b"], mix, seq_len)
    z1 = _mixout_call(mix, w["w_out"], x)
    x2 = _xattn_call(z1, w["ln1_g"], w["ln1_b"], w["xq_w"], kx, vx, w["xo_w"], w["ln2_g"], w["ln2_b"],
                     seq_len)
    return _ffn_call(x2, w["w_gate"], w["w_up"], w["w_down"], w["ln3_g"], w["ln3_b"])


def kernel(x_prompt, x_sample, mem_prompt, mem_sample, rel_bias, w_in, conv_w, conv_b, conv_ln_g,
           conv_ln_b, sink, w_out, ln1_g, ln1_b, xq_w, xk_w, xv_w, xo_w, ln2_g, ln2_b, w_gate, w_up,
           w_down, ln3_g, ln3_b):
    bp, sp, _ = x_prompt.shape
    bs, ss, _ = x_sample.shape
    xs = [x_prompt.reshape(bp * sp, D_MODEL), x_sample.reshape(bs * ss, D_MODEL)]
    mems = [mem_prompt, mem_sample]
    bias = _bias_call(rel_bias)
    for l in range(DEPTH):
        w = {
            "w_in": w_in[l].astype(BF16), "conv_w": conv_w[l], "conv_b": conv_b[l],
            "conv_ln_g": conv_ln_g[l], "conv_ln_b": conv_ln_b[l], "sink": sink[l],
            "w_out": w_out[l].astype(BF16), "ln1_g": ln1_g[l], "ln1_b": ln1_b[l],
            "xq_w": xq_w[l].astype(BF16), "xo_w": xo_w[l].astype(BF16),
            "ln2_g": ln2_g[l], "ln2_b": ln2_b[l],
            "w_gate": w_gate[l].astype(BF16), "w_up": w_up[l].astype(BF16),
            "w_down": w_down[l].astype(BF16), "ln3_g": ln3_g[l], "ln3_b": ln3_b[l],
        }
        wkv = jnp.concatenate([xk_w[l], xv_w[l]], axis=-1).astype(BF16)
        memb = jnp.concatenate([m.reshape(-1, D_MODEL) for m in mems], axis=0).astype(BF16)
        kv = _mm_call(memb, wkv, 0, 2 * X_WIDTH, MEM_LEN, X_WIDTH, "memkv")
        kv = kv.reshape(bp + bs, MEM_LEN, 2 * X_WIDTH)
        kx, vx = kv[..., :X_WIDTH], kv[..., X_WIDTH:]
        xs = [
            _layer(xs[0], kx[:bp], vx[:bp], bias, w, sp),
            _layer(xs[1], kx[bp:], vx[bp:], bias, w, ss),
        ]
    return (xs[0].reshape(bp, sp, D_MODEL), xs[1].reshape(bs, ss, D_MODEL))
```

```python
import functools
import math

import jax
import jax.numpy as jnp
from jax import lax
from jax.experimental import pallas as pl
from jax.experimental.pallas import tpu as pltpu

D_MODEL = 4096
HEAD_DIM = 128
CONV_WIDTH = 2048
ATTN_WIDTH = 2048
N_HEADS = 16
N_KV_HEADS = 4
GROUP = 4
KV_WIDTH = 512
CONV_K = 31
WINDOW = 128
BLOCK = 128
N_BUCKETS = 32
MAX_DISTANCE = 128
MEM_LEN = 256
X_HEADS = 4
X_WIDTH = 512
D_FF = 11008
DEPTH = 1
ALPHA = (2 * DEPTH) ** 0.25
LN_EPS = 1e-5
NEG_INF = -1e30
SCALE = HEAD_DIM ** -0.5
LOG2E = math.log2(math.e)

V7X_VMEM_BYTES = 64 * 1024 * 1024
VMEM_LIMIT = V7X_VMEM_BYTES - 8 * 1024 * 1024
LANES = 128
CONV_HALO = 16

BF16 = jnp.bfloat16
F32 = jnp.float32


def _params(sem):
    return pltpu.CompilerParams(dimension_semantics=sem, vmem_limit_bytes=VMEM_LIMIT)


def _dot(a, b):
    return jnp.dot(a, b, preferred_element_type=F32)


def _dot_nt(a, b):
    return lax.dot_general(a, b, (((1,), (1,)), ((), ())), preferred_element_type=F32)


LN_ROWS = 128


def _ln_rows(z, g, b):
    mu = jnp.mean(z, axis=-1, keepdims=True)
    zc = z - mu
    var = jnp.mean(zc * zc, axis=-1, keepdims=True)
    return zc * lax.rsqrt(var + LN_EPS) * g + b


def _ln_loop(nrows, load, store, g_ref, b_ref):
    def rows(r, carry):
        r0 = pl.multiple_of(r * LN_ROWS, LN_ROWS)
        store(r0, _ln_rows(load(r0), g_ref[...], b_ref[...]))
        return carry

    lax.fori_loop(0, nrows // LN_ROWS, rows, 0)


def _glu_body(x_ref, wv_ref, wg_ref, o_ref):
    x = x_ref[...]
    v = _dot(x, wv_ref[...])
    g = _dot(x, wg_ref[...])
    o_ref[...] = v * jax.nn.sigmoid(g)


def _glu_call(xb, w_in):
    t = xb.shape[0]
    bm, bn = 1024, 512
    nb = CONV_WIDTH // bn
    return pl.pallas_call(
        _glu_body,
        grid=(t // bm, nb),
        in_specs=[
            pl.BlockSpec((bm, D_MODEL), lambda i, j: (i, 0)),
            pl.BlockSpec((D_MODEL, bn), lambda i, j: (0, j)),
            pl.BlockSpec((D_MODEL, bn), lambda i, j: (0, j + nb)),
        ],
        out_specs=pl.BlockSpec((bm, bn), lambda i, j: (i, j)),
        out_shape=jax.ShapeDtypeStruct((t, CONV_WIDTH), F32),
        compiler_params=_params(("parallel", "arbitrary")),
        name="glu",
    )(xb, w_in, w_in)


def _mm_body(x_ref, w_ref, o_ref):
    o_ref[...] = _dot(x_ref[...], w_ref[...]).astype(o_ref.dtype)


def _mm_call(xb, w, col_off, n, bm, bn, name):
    t, k = xb.shape
    off = col_off // bn
    return pl.pallas_call(
        _mm_body,
        grid=(t // bm, n // bn),
        in_specs=[
            pl.BlockSpec((bm, k), lambda i, j: (i, 0)),
            pl.BlockSpec((k, bn), lambda i, j: (0, j + off)),
        ],
        out_specs=pl.BlockSpec((bm, bn), lambda i, j: (i, j)),
        out_shape=jax.ShapeDtypeStruct((t, n), BF16),
        compiler_params=_params(("parallel", "arbitrary")),
        name=name,
    )(xb, w)


def _bias_body(bucket_ref, rb_ref, o_ref):
    h = pl.program_id(0)
    bkt = bucket_ref[...]
    acc = jnp.zeros(bkt.shape, F32)
    for b in range(N_BUCKETS):
        acc = jnp.where(bkt == b, rb_ref[b, h], acc)
    o_ref[0] = acc * LOG2E


def _t5_bucket(rel):
    nb = N_BUCKETS // 2
    max_exact = nb // 2
    ret = jnp.where(rel > 0, nb, 0)
    n = jnp.abs(rel)
    nf = jnp.maximum(n, 1).astype(F32)
    large = max_exact + (jnp.log(nf / max_exact) / math.log(MAX_DISTANCE / max_exact)
                         * (nb - max_exact)).astype(jnp.int32)
    large = jnp.minimum(large, nb - 1)
    return ret + jnp.where(n < max_exact, n, large)


def _bias_call(rel_bias):
    rel = jnp.arange(3 * BLOCK)[:, None] - BLOCK - jnp.arange(BLOCK)[None, :]
    bucket = _t5_bucket(rel).astype(jnp.int32)
    return pl.pallas_call(
        _bias_body,
        grid=(N_HEADS,),
        in_specs=[
            pl.BlockSpec((3 * BLOCK, BLOCK), lambda h: (0, 0)),
            pl.BlockSpec(memory_space=pltpu.SMEM),
        ],
        out_specs=pl.BlockSpec((1, 3 * BLOCK, BLOCK), lambda h: (h, 0, 0)),
        out_shape=jax.ShapeDtypeStruct((N_HEADS, 3 * BLOCK, BLOCK), F32),
        compiler_params=_params(("arbitrary",)),
        name="t5bias",
    )(bucket, rel_bias)


ATTN_TQ = 512
ATTN_QB = ATTN_TQ // BLOCK


def _attn_body(seq_len, q_ref, kp_ref, km_ref, kn_ref, vp_ref, vm_ref, vn_ref, bias_ref, sink_ref,
               o_ref):
    i = pl.program_id(0)
    t0 = i * ATTN_TQ
    key = lax.broadcasted_iota(jnp.int32, (3 * BLOCK, GROUP * BLOCK), 0)
    qry = lax.broadcasted_iota(jnp.int32, (3 * BLOCK, GROUP * BLOCK), 1) & (BLOCK - 1)
    band = (key >= qry) & (key <= qry + 2 * WINDOW)
    for kh in range(N_KV_HEADS):
        cs = slice(kh * HEAD_DIM, (kh + 1) * HEAD_DIM)
        kall = jnp.concatenate([kp_ref[:, cs], km_ref[:, cs], kn_ref[:, cs]], axis=0)
        vt = jnp.concatenate([vp_ref[:, cs], vm_ref[:, cs], vn_ref[:, cs]], axis=0).T
        bias = jnp.concatenate([bias_ref[kh * GROUP + g] for g in range(GROUP)], axis=1)
        sk = jnp.concatenate(
            [jnp.full((1, BLOCK), sink_ref[kh * GROUP + g] * LOG2E, F32) for g in range(GROUP)], axis=1)
        for qb in range(ATTN_QB):
            rs = slice(qb * BLOCK, (qb + 1) * BLOCK)
            pos = t0 + qb * BLOCK
            first = lax.rem(pos, seq_len) == 0
            last = lax.rem(pos + BLOCK, seq_len) == 0
            lo = jnp.where(first, BLOCK, 0)
            hi = jnp.where(last, 2 * BLOCK, 3 * BLOCK)
            valid = band & (key >= lo) & (key < hi)
            q = jnp.concatenate(
                [q_ref[rs, (kh * GROUP + g) * HEAD_DIM:(kh * GROUP + g + 1) * HEAD_DIM]
                 for g in range(GROUP)], axis=0)
            kb = kall[qb * BLOCK:(qb + 3) * BLOCK]
            s = _dot_nt(kb, q) * (SCALE * LOG2E)
            s = jnp.where(valid, s + bias, NEG_INF)
            m = jnp.maximum(jnp.max(s, axis=0, keepdims=True), sk)
            p = jnp.exp2(s - m)
            denom = jnp.sum(p, axis=0, keepdims=True) + jnp.exp2(sk - m)
            p = (p * (1.0 / denom)).astype(BF16)
            ot = _dot(vt[:, qb * BLOCK:(qb + 3) * BLOCK], p)
            for g in range(GROUP):
                h = kh * GROUP + g
                o_ref[rs, h * HEAD_DIM:(h + 1) * HEAD_DIM] = (
                    ot[:, g * BLOCK:(g + 1) * BLOCK].T.astype(BF16))


def _attn_call(qkv, bias, sink, seq_len):
    t = qkv.shape[0]
    nblk = t // BLOCK
    kcol = ATTN_WIDTH // KV_WIDTH
    vcol = kcol + 1

    def prev_map(c):
        return lambda i: (jnp.maximum(i * ATTN_QB - 1, 0), c)

    def next_map(c):
        return lambda i: (jnp.minimum((i + 1) * ATTN_QB, nblk - 1), c)

    return pl.pallas_call(
        functools.partial(_attn_body, seq_len),
        grid=(t // ATTN_TQ,),
        in_specs=[
            pl.BlockSpec((ATTN_TQ, ATTN_WIDTH), lambda i: (i, 0)),
            pl.BlockSpec((BLOCK, KV_WIDTH), prev_map(kcol)),
            pl.BlockSpec((ATTN_TQ, KV_WIDTH), lambda i: (i, kcol)),
            pl.BlockSpec((BLOCK, KV_WIDTH), next_map(kcol)),
            pl.BlockSpec((BLOCK, KV_WIDTH), prev_map(vcol)),
            pl.BlockSpec((ATTN_TQ, KV_WIDTH), lambda i: (i, vcol)),
            pl.BlockSpec((BLOCK, KV_WIDTH), next_map(vcol)),
            pl.BlockSpec((N_HEADS, 3 * BLOCK, BLOCK), lambda i: (0, 0, 0)),
            pl.BlockSpec(memory_space=pltpu.SMEM),
        ],
        out_specs=pl.BlockSpec((ATTN_TQ, ATTN_WIDTH), lambda i: (i, 0)),
        out_shape=jax.ShapeDtypeStruct((t, ATTN_WIDTH), BF16),
        compiler_params=_params(("parallel",)),
        name="attn",
    )(qkv, qkv, qkv, qkv, qkv, qkv, qkv, bias, sink)


CONV_TS = 256
CONV_RC = 64
CONV_NC = CONV_WIDTH // LANES
SUBLANES = 8


def _conv_body(seq_len, up_ref, um_ref, un_ref, w_ref, cb_ref, g_ref, b_ref, o_ref, buf_ref, h_ref):
    i = pl.program_id(0)
    t0 = i * CONV_TS
    first = lax.rem(t0, seq_len) == 0
    last = lax.rem(t0 + CONV_TS, seq_len) == 0
    shift = CONV_HALO - CONV_K // 2
    groups = CONV_RC // SUBLANES

    for c in range(CONV_NC):
        cs = slice(c * LANES, (c + 1) * LANES)
        buf_ref[c, 0:CONV_HALO, :] = jnp.where(first, 0.0, up_ref[:, cs])
        buf_ref[c, CONV_HALO:CONV_HALO + CONV_TS, :] = um_ref[:, cs]
        buf_ref[c, CONV_HALO + CONV_TS:, :] = jnp.where(last, 0.0, un_ref[:, cs])

        def rows(r, carry, c=c, cs=cs):
            r0 = pl.multiple_of(r * CONV_RC, CONV_RC)
            win = [buf_ref[c, pl.ds(r0 + shift + off, SUBLANES), :]
                   for off in range(CONV_K + CONV_RC - SUBLANES)]
            accs = [cb_ref[:, cs]] * groups
            for k in range(CONV_K):
                wk = w_ref[k * SUBLANES:(k + 1) * SUBLANES, cs]
                accs = [accs[j] + win[k + SUBLANES * j] * wk for j in range(groups)]
            for j in range(groups):
                h_ref[pl.ds(r0 + SUBLANES * j, SUBLANES), cs] = accs[j]
            return carry

        lax.fori_loop(0, CONV_TS // CONV_RC, rows, 0)

    def store(r0, y):
        o_ref[pl.ds(r0, LN_ROWS), :] = (y * jax.nn.sigmoid(y)).astype(BF16)

    _ln_loop(CONV_TS, lambda r0: h_ref[pl.ds(r0, LN_ROWS), :], store, g_ref, b_ref)


def _conv_call(u, conv_w, conv_b, ln_g, ln_b, seq_len):
    t = u.shape[0]
    per = CONV_TS // CONV_HALO
    nh = t // CONV_HALO
    w_rep = jnp.repeat(conv_w, SUBLANES, axis=0)
    cb_rep = jnp.broadcast_to(conv_b[None, :], (SUBLANES, CONV_WIDTH))
    full = lambda a: pl.BlockSpec(a.shape, lambda i: (0, 0))
    g2, b2 = ln_g.reshape(1, CONV_WIDTH), ln_b.reshape(1, CONV_WIDTH)
    return pl.pallas_call(
        functools.partial(_conv_body, seq_len),
        grid=(t // CONV_TS,),
        in_specs=[
            pl.BlockSpec((CONV_HALO, CONV_WIDTH), lambda i: (jnp.maximum(i * per - 1, 0), 0)),
            pl.BlockSpec((CONV_TS, CONV_WIDTH), lambda i: (i, 0)),
            pl.BlockSpec((CONV_HALO, CONV_WIDTH), lambda i: (jnp.minimum((i + 1) * per, nh - 1), 0)),
            full(w_rep), full(cb_rep), full(g2), full(b2),
        ],
        out_specs=pl.BlockSpec((CONV_TS, CONV_WIDTH), lambda i: (i, 0)),
        out_shape=jax.ShapeDtypeStruct((t, CONV_WIDTH), BF16),
        scratch_shapes=[
            pltpu.VMEM((CONV_NC, CONV_TS + 2 * CONV_HALO, LANES), F32),
            pltpu.VMEM((CONV_TS, CONV_WIDTH), F32),
        ],
        compiler_params=_params(("parallel",)),
        name="conv",
    )(u, u, u, w_rep, cb_rep, g2, b2)


MIX_BM = 1024
MIX_BN = 512


def _mixout_body(c_ref, a_ref, wc_ref, wa_ref, x_ref, o_ref):
    mix = _dot(c_ref[...], wc_ref[...]) + _dot(a_ref[...], wa_ref[...])
    o_ref[...] = ALPHA * x_ref[...] + mix


def _mixout_call(conv_out, attn_out, w_out, x):
    t = x.shape[0]
    return pl.pallas_call(
        _mixout_body,
        grid=(t // MIX_BM, D_MODEL // MIX_BN),
        in_specs=[
            pl.BlockSpec((MIX_BM, CONV_WIDTH), lambda i, j: (i, 0)),
            pl.BlockSpec((MIX_BM, ATTN_WIDTH), lambda i, j: (i, 0)),
            pl.BlockSpec((CONV_WIDTH, MIX_BN), lambda i, j: (0, j)),
            pl.BlockSpec((ATTN_WIDTH, MIX_BN), lambda i, j: (1, j)),
            pl.BlockSpec((MIX_BM, MIX_BN), lambda i, j: (i, j)),
        ],
        out_specs=pl.BlockSpec((MIX_BM, MIX_BN), lambda i, j: (i, j)),
        out_shape=jax.ShapeDtypeStruct((t, D_MODEL), F32),
        compiler_params=_params(("parallel", "arbitrary")),
        name="mixout",
    )(conv_out, attn_out, w_out, w_out, x)


XA_BM = 256


def _xattn_body(z_ref, g1_ref, b1_ref, wq_ref, k_ref, v_ref, wo_ref, g2_ref, b2_ref, o_ref, ob_ref,
                x1_ref, x1b_ref, z2_ref):
    def store_x1(r0, x1):
        x1_ref[pl.ds(r0, LN_ROWS), :] = x1
        x1b_ref[pl.ds(r0, LN_ROWS), :] = x1.astype(BF16)

    _ln_loop(XA_BM, lambda r0: z_ref[pl.ds(r0, LN_ROWS), :], store_x1, g1_ref, b1_ref)

    q = _dot(x1b_ref[...], wq_ref[...])
    outs = []
    for h in range(X_HEADS):
        cs = slice(h * HEAD_DIM, (h + 1) * HEAD_DIM)
        s = _dot_nt(q[:, cs].astype(BF16), k_ref[0, :, cs]) * SCALE
        m = jnp.max(s, axis=-1, keepdims=True)
        e = jnp.exp(s - m)
        p = (e / jnp.sum(e, axis=-1, keepdims=True)).astype(BF16)
        outs.append(_dot(p, v_ref[0, :, cs]))
    o = jnp.concatenate(outs, axis=-1).astype(BF16)
    z2_ref[...] = ALPHA * x1_ref[...] + _dot(o, wo_ref[...])

    def store_x2(r0, x2):
        o_ref[pl.ds(r0, LN_ROWS), :] = x2
        ob_ref[pl.ds(r0, LN_ROWS), :] = x2.astype(BF16)

    _ln_loop(XA_BM, lambda r0: z2_ref[pl.ds(r0, LN_ROWS), :], store_x2, g2_ref, b2_ref)


def _xattn_call(z1, ln1_g, ln1_b, xq_w, kx, vx, xo_w, ln2_g, ln2_b, seq_len):
    t = z1.shape[0]
    per = seq_len // XA_BM
    vec = pl.BlockSpec((1, D_MODEL), lambda i: (0, 0))
    return pl.pallas_call(
        _xattn_body,
        grid=(t // XA_BM,),
        in_specs=[
            pl.BlockSpec((XA_BM, D_MODEL), lambda i: (i, 0)),
            vec, vec,
            pl.BlockSpec((D_MODEL, X_WIDTH), lambda i: (0, 0)),
            pl.BlockSpec((1, MEM_LEN, X_WIDTH), lambda i: (i // per, 0, 0)),
            pl.BlockSpec((1, MEM_LEN, X_WIDTH), lambda i: (i // per, 0, 0)),
            pl.BlockSpec((X_WIDTH, D_MODEL), lambda i: (0, 0)),
            vec, vec,
        ],
        out_specs=[
            pl.BlockSpec((XA_BM, D_MODEL), lambda i: (i, 0)),
            pl.BlockSpec((XA_BM, D_MODEL), lambda i: (i, 0)),
        ],
        out_shape=[
            jax.ShapeDtypeStruct((t, D_MODEL), F32),
            jax.ShapeDtypeStruct((t, D_MODEL), BF16),
        ],
        scratch_shapes=[
            pltpu.VMEM((XA_BM, D_MODEL), F32),
            pltpu.VMEM((XA_BM, D_MODEL), BF16),
            pltpu.VMEM((XA_BM, D_MODEL), F32),
        ],
        compiler_params=_params(("parallel",)),
        name="xattn",
    )(z1, ln1_g.reshape(1, D_MODEL), ln1_b.reshape(1, D_MODEL), xq_w, kx, vx, xo_w,
      ln2_g.reshape(1, D_MODEL), ln2_b.reshape(1, D_MODEL))


FFN_BM = 1024
FFN_TF = 256
FFN_NJ = D_FF // FFN_TF
FFN_EP = FFN_BM // LN_ROWS
FFN_ZERO_ROWS = 64


def _ffn_body(xb_ref, wg_ref, wu_ref, wd_ref, g_ref, b_ref, x_hbm, o_hbm, acc_ref, xres_ref, xsem, osem):
    i = pl.program_id(0)
    j = pl.program_id(1)
    row0 = i * FFN_BM

    def x_copy(c):
        return pltpu.make_async_copy(
            x_hbm.at[pl.ds(row0 + c * LN_ROWS, LN_ROWS), :], xres_ref.at[c % 2], xsem.at[c % 2])

    def o_copy(c):
        return pltpu.make_async_copy(
            acc_ref.at[pl.ds(c * LN_ROWS, LN_ROWS), :],
            o_hbm.at[pl.ds(row0 + c * LN_ROWS, LN_ROWS), :], osem.at[c])

    @pl.when(j == 0)
    def _():
        def rows(r, carry):
            r0 = pl.multiple_of(r * FFN_ZERO_ROWS, FFN_ZERO_ROWS)
            acc_ref[pl.ds(r0, FFN_ZERO_ROWS), :] = jnp.zeros((FFN_ZERO_ROWS, D_MODEL), F32)
            return carry

        lax.fori_loop(0, FFN_BM // FFN_ZERO_ROWS, rows, 0)

    @pl.when(j == FFN_NJ - 1)
    def _():
        x_copy(0).start()
        x_copy(1).start()

    xb = xb_ref[...]
    gate = _dot(xb, wg_ref[...])
    up = _dot(xb, wu_ref[...])
    a = (gate * jax.nn.sigmoid(gate) * up).astype(BF16)
    acc_ref[...] += _dot(a, wd_ref[...])

    @pl.when(j == FFN_NJ - 1)
    def _():
        for c in range(FFN_EP):
            rs = slice(c * LN_ROWS, (c + 1) * LN_ROWS)
            x_copy(c).wait()
            z = ALPHA * xres_ref[c % 2] + acc_ref[rs, :]
            acc_ref[rs, :] = _ln_rows(z, g_ref[...], b_ref[...])
            if c + 2 < FFN_EP:
                x_copy(c + 2).start()
            o_copy(c).start()
        for c in range(FFN_EP):
            o_copy(c).wait()


def _ffn_call(x2, x2b, w_gate, w_up, w_down, ln_g, ln_b):
    t = x2.shape[0]
    return pl.pallas_call(
        _ffn_body,
        grid=(t // FFN_BM, FFN_NJ),
        in_specs=[
            pl.BlockSpec((FFN_BM, D_MODEL), lambda i, j: (i, 0)),
            pl.BlockSpec((D_MODEL, FFN_TF), lambda i, j: (0, j)),
            pl.BlockSpec((D_MODEL, FFN_TF), lambda i, j: (0, j)),
            pl.BlockSpec((FFN_TF, D_MODEL), lambda i, j: (j, 0)),
            pl.BlockSpec((1, D_MODEL), lambda i, j: (0, 0)),
            pl.BlockSpec((1, D_MODEL), lambda i, j: (0, 0)),
            pl.BlockSpec(memory_space=pl.ANY),
        ],
        out_specs=pl.BlockSpec(memory_space=pl.ANY),
        out_shape=jax.ShapeDtypeStruct((t, D_MODEL), F32),
        scratch_shapes=[
            pltpu.VMEM((FFN_BM, D_MODEL), F32),
            pltpu.VMEM((2, LN_ROWS, D_MODEL), F32),
            pltpu.SemaphoreType.DMA((2,)),
            pltpu.SemaphoreType.DMA((FFN_EP,)),
        ],
        compiler_params=_params(("arbitrary", "arbitrary")),
        name="ffn",
    )(x2b, w_gate, w_up, w_down, ln_g.reshape(1, D_MODEL), ln_b.reshape(1, D_MODEL), x2)


def _layer(x, kx, vx, bias, w, seq_len):
    xb = x.astype(BF16)
    u = _glu_call(xb, w["w_in"])
    qkv = _mm_call(xb, w["w_in"], 2 * CONV_WIDTH, ATTN_WIDTH + 2 * KV_WIDTH, 1024, 512, "qkv")
    attn_out = _attn_call(qkv, bias, w["sink"], seq_len)
    conv_out = _conv_call(u, w["conv_w"], w["conv_b"], w["conv_ln_g"], w["conv_ln_b"], seq_len)
    z1 = _mixout_call(conv_out, attn_out, w["w_out"], x)
    x2, x2b = _xattn_call(z1, w["ln1_g"], w["ln1_b"], w["xq_w"], kx, vx, w["xo_w"], w["ln2_g"],
                          w["ln2_b"], seq_len)
    return _ffn_call(x2, x2b, w["w_gate"], w["w_up"], w["w_down"], w["ln3_g"], w["ln3_b"])


def kernel(x_prompt, x_sample, mem_prompt, mem_sample, rel_bias, w_in, conv_w, conv_b, conv_ln_g,
           conv_ln_b, sink, w_out, ln1_g, ln1_b, xq_w, xk_w, xv_w, xo_w, ln2_g, ln2_b, w_gate, w_up,
           w_down, ln3_g, ln3_b):
    bp, sp, _ = x_prompt.shape
    bs, ss, _ = x_sample.shape
    xs = [x_prompt.reshape(bp * sp, D_MODEL), x_sample.reshape(bs * ss, D_MODEL)]
    mems = [mem_prompt, mem_sample]
    bias = _bias_call(rel_bias)
    for l in range(DEPTH):
        w = {
            "w_in": w_in[l].astype(BF16), "conv_w": conv_w[l], "conv_b": conv_b[l],
            "conv_ln_g": conv_ln_g[l], "conv_ln_b": conv_ln_b[l], "sink": sink[l],
            "w_out": w_out[l].astype(BF16), "ln1_g": ln1_g[l], "ln1_b": ln1_b[l],
            "xq_w": xq_w[l].astype(BF16), "xo_w": xo_w[l].astype(BF16),
            "ln2_g": ln2_g[l], "ln2_b": ln2_b[l],
            "w_gate": w_gate[l].astype(BF16), "w_up": w_up[l].astype(BF16),
            "w_down": w_down[l].astype(BF16), "ln3_g": ln3_g[l], "ln3_b": ln3_b[l],
        }
        wkv = jnp.concatenate([xk_w[l], xv_w[l]], axis=-1).astype(BF16)
        memb = jnp.concatenate([m.reshape(-1, D_MODEL) for m in mems], axis=0).astype(BF16)
        kv = _mm_call(memb, wkv, 0, 2 * X_WIDTH, MEM_LEN, X_WIDTH, "memkv")
        kv = kv.reshape(bp + bs, MEM_LEN, 2 * X_WIDTH)
        kx, vx = kv[..., :X_WIDTH], kv[..., X_WIDTH:]
        xs = [
            _layer(xs[0], kx[:bp], vx[:bp], bias, w, sp),
            _layer(xs[1], kx[bp:], vx[bp:], bias, w, ss),
        ]
    return (xs[0].reshape(bp, sp, D_MODEL), xs[1].reshape(bs, ss, D_MODEL))
```

```python
import functools
import math

import jax
import jax.numpy as jnp
from jax import lax
from jax.experimental import pallas as pl
from jax.experimental.pallas import tpu as pltpu

D_MODEL = 4096
HEAD_DIM = 128
CONV_WIDTH = 2048
ATTN_WIDTH = 2048
N_HEADS = 16
N_KV_HEADS = 4
GROUP = 4
KV_WIDTH = 512
CONV_K = 31
WINDOW = 128
BLOCK = 128
N_BUCKETS = 32
MAX_DISTANCE = 128
MEM_LEN = 256
X_HEADS = 4
X_WIDTH = 512
D_FF = 11008
DEPTH = 1
ALPHA = (2 * DEPTH) ** 0.25
LN_EPS = 1e-5
NEG_INF = -1e30
SCALE = HEAD_DIM ** -0.5
LOG2E = math.log2(math.e)

V7X_VMEM_BYTES = 64 * 1024 * 1024
VMEM_LIMIT = V7X_VMEM_BYTES - 8 * 1024 * 1024
LANES = 128
SUBLANES = 8
CONV_HALO = 16

BF16 = jnp.bfloat16
F32 = jnp.float32


def _params(sem):
    return pltpu.CompilerParams(dimension_semantics=sem, vmem_limit_bytes=VMEM_LIMIT)


def _dot(a, b):
    return jnp.dot(a, b, preferred_element_type=F32)


def _dot_nt(a, b):
    return lax.dot_general(a, b, (((1,), (1,)), ((), ())), preferred_element_type=F32)


LN_ROWS = 128


def _ln_rows(z, g, b):
    mu = jnp.mean(z, axis=-1, keepdims=True)
    zc = z - mu
    var = jnp.mean(zc * zc, axis=-1, keepdims=True)
    return zc * lax.rsqrt(var + LN_EPS) * g + b


def _ln_loop(nrows, load, store, g_ref, b_ref):
    def rows(r, carry):
        r0 = pl.multiple_of(r * LN_ROWS, LN_ROWS)
        store(r0, _ln_rows(load(r0), g_ref[...], b_ref[...]))
        return carry

    lax.fori_loop(0, nrows // LN_ROWS, rows, 0)


PROJ_BM = 1024
PROJ_BN = 512
PROJ_ROWS = 256


def _glu_body(x_ref, wv_ref, wg_ref, o_ref):
    for r0 in range(0, x_ref.shape[0], PROJ_ROWS):
        x = x_ref[r0:r0 + PROJ_ROWS, :]
        v = _dot(x, wv_ref[...])
        g = _dot(x, wg_ref[...])
        o_ref[r0:r0 + PROJ_ROWS, :] = v * jax.nn.sigmoid(g)


def _glu_call(xb, w_in):
    t = xb.shape[0]
    nb = CONV_WIDTH // PROJ_BN
    return pl.pallas_call(
        _glu_body,
        grid=(t // PROJ_BM, nb),
        in_specs=[
            pl.BlockSpec((PROJ_BM, D_MODEL), lambda i, j: (i, 0)),
            pl.BlockSpec((D_MODEL, PROJ_BN), lambda i, j: (0, j)),
            pl.BlockSpec((D_MODEL, PROJ_BN), lambda i, j: (0, j + nb)),
        ],
        out_specs=pl.BlockSpec((PROJ_BM, PROJ_BN), lambda i, j: (i, j)),
        out_shape=jax.ShapeDtypeStruct((t, CONV_WIDTH), F32),
        compiler_params=_params(("parallel", "arbitrary")),
        name="glu",
    )(xb, w_in, w_in)


QKV_BM = 512
QKV_BN = 1024
CAST_ROWS = 32


def _qkv_body(x_ref, w_ref, o_ref, xb_ref):
    @pl.when(pl.program_id(1) == 0)
    def _():
        def rows(r, carry):
            r0 = pl.multiple_of(r * CAST_ROWS, CAST_ROWS)
            xb_ref[pl.ds(r0, CAST_ROWS), :] = x_ref[pl.ds(r0, CAST_ROWS), :].astype(BF16)
            return carry

        lax.fori_loop(0, QKV_BM // CAST_ROWS, rows, 0)

    o_ref[...] = _dot(xb_ref[...], w_ref[...]).astype(BF16)


def _qkv_call(x, w_in):
    t = x.shape[0]
    n = ATTN_WIDTH + 2 * KV_WIDTH
    off = 2 * CONV_WIDTH // QKV_BN
    return pl.pallas_call(
        _qkv_body,
        grid=(t // QKV_BM, n // QKV_BN),
        in_specs=[
            pl.BlockSpec((QKV_BM, D_MODEL), lambda i, j: (i, 0)),
            pl.BlockSpec((D_MODEL, QKV_BN), lambda i, j: (0, j + off)),
        ],
        out_specs=[
            pl.BlockSpec((QKV_BM, QKV_BN), lambda i, j: (i, j)),
            pl.BlockSpec((QKV_BM, D_MODEL), lambda i, j: (i, 0)),
        ],
        out_shape=[
            jax.ShapeDtypeStruct((t, n), BF16),
            jax.ShapeDtypeStruct((t, D_MODEL), BF16),
        ],
        compiler_params=_params(("parallel", "arbitrary")),
        name="qkv",
    )(x, w_in)


def _mm_body(x_ref, w_ref, o_ref):
    o_ref[...] = _dot(x_ref[...], w_ref[...]).astype(o_ref.dtype)


def _mm_call(xb, w, col_off, n, bm, bn, name):
    t, k = xb.shape
    off = col_off // bn
    return pl.pallas_call(
        _mm_body,
        grid=(t // bm, n // bn),
        in_specs=[
            pl.BlockSpec((bm, k), lambda i, j: (i, 0)),
            pl.BlockSpec((k, bn), lambda i, j: (0, j + off)),
        ],
        out_specs=pl.BlockSpec((bm, bn), lambda i, j: (i, j)),
        out_shape=jax.ShapeDtypeStruct((t, n), BF16),
        compiler_params=_params(("parallel", "arbitrary")),
        name=name,
    )(xb, w)


def _bias_body(bucket_ref, rb_ref, o_ref):
    h = pl.program_id(0)
    bkt = bucket_ref[...]
    acc = jnp.zeros(bkt.shape, F32)
    for b in range(N_BUCKETS):
        acc = jnp.where(bkt == b, rb_ref[b, h], acc)
    o_ref[0] = acc * LOG2E


def _t5_bucket(rel):
    nb = N_BUCKETS // 2
    max_exact = nb // 2
    ret = jnp.where(rel > 0, nb, 0)
    n = jnp.abs(rel)
    nf = jnp.maximum(n, 1).astype(F32)
    large = max_exact + (jnp.log(nf / max_exact) / math.log(MAX_DISTANCE / max_exact)
                         * (nb - max_exact)).astype(jnp.int32)
    large = jnp.minimum(large, nb - 1)
    return ret + jnp.where(n < max_exact, n, large)


def _bias_call(rel_bias):
    rel = jnp.arange(3 * BLOCK)[:, None] - BLOCK - jnp.arange(BLOCK)[None, :]
    bucket = _t5_bucket(rel).astype(jnp.int32)
    return pl.pallas_call(
        _bias_body,
        grid=(N_HEADS,),
        in_specs=[
            pl.BlockSpec((3 * BLOCK, BLOCK), lambda h: (0, 0)),
            pl.BlockSpec(memory_space=pltpu.SMEM),
        ],
        out_specs=pl.BlockSpec((1, 3 * BLOCK, BLOCK), lambda h: (h, 0, 0)),
        out_shape=jax.ShapeDtypeStruct((N_HEADS, 3 * BLOCK, BLOCK), F32),
        compiler_params=_params(("arbitrary",)),
        name="t5bias",
    )(bucket, rel_bias)


ATTN_TQ = 512
ATTN_QB = ATTN_TQ // BLOCK


def _attn_body(seq_len, q_ref, kp_ref, km_ref, kn_ref, vp_ref, vm_ref, vn_ref, bias_ref, sink_ref,
               o_ref):
    i = pl.program_id(0)
    t0 = i * ATTN_TQ
    key = lax.broadcasted_iota(jnp.int32, (3 * BLOCK, GROUP * BLOCK), 0)
    qry = lax.broadcasted_iota(jnp.int32, (3 * BLOCK, GROUP * BLOCK), 1) & (BLOCK - 1)
    band = (key >= qry) & (key <= qry + 2 * WINDOW)
    for kh in range(N_KV_HEADS):
        cs = slice(kh * HEAD_DIM, (kh + 1) * HEAD_DIM)
        kall = jnp.concatenate([kp_ref[:, cs], km_ref[:, cs], kn_ref[:, cs]], axis=0)
        vt = jnp.concatenate([vp_ref[:, cs], vm_ref[:, cs], vn_ref[:, cs]], axis=0).T
        bias = jnp.concatenate([bias_ref[kh * GROUP + g] for g in range(GROUP)], axis=1)
        sk = jnp.concatenate(
            [jnp.full((1, BLOCK), sink_ref[kh * GROUP + g] * LOG2E, F32) for g in range(GROUP)], axis=1)
        for qb in range(ATTN_QB):
            rs = slice(qb * BLOCK, (qb + 1) * BLOCK)
            pos = t0 + qb * BLOCK
            first = lax.rem(pos, seq_len) == 0
            last = lax.rem(pos + BLOCK, seq_len) == 0
            lo = jnp.where(first, BLOCK, 0)
            hi = jnp.where(last, 2 * BLOCK, 3 * BLOCK)
            valid = band & (key >= lo) & (key < hi)
            q = jnp.concatenate(
                [q_ref[rs, (kh * GROUP + g) * HEAD_DIM:(kh * GROUP + g + 1) * HEAD_DIM]
                 for g in range(GROUP)], axis=0)
            kb = kall[qb * BLOCK:(qb + 3) * BLOCK]
            s = _dot_nt(kb, q) * (SCALE * LOG2E)
            s = jnp.where(valid, s + bias, NEG_INF)
            m = jnp.maximum(jnp.max(s, axis=0, keepdims=True), sk)
            p = jnp.exp2(s - m)
            denom = jnp.sum(p, axis=0, keepdims=True) + jnp.exp2(sk - m)
            p = (p * (1.0 / denom)).astype(BF16)
            ot = _dot(vt[:, qb * BLOCK:(qb + 3) * BLOCK], p)
            for g in range(GROUP):
                h = kh * GROUP + g
                o_ref[rs, h * HEAD_DIM:(h + 1) * HEAD_DIM] = (
                    ot[:, g * BLOCK:(g + 1) * BLOCK].T.astype(BF16))


def _attn_call(qkv, bias, sink, seq_len):
    t = qkv.shape[0]
    nblk = t // BLOCK
    kcol = ATTN_WIDTH // KV_WIDTH
    vcol = kcol + 1

    def prev_map(c):
        return lambda i: (jnp.maximum(i * ATTN_QB - 1, 0), c)

    def next_map(c):
        return lambda i: (jnp.minimum((i + 1) * ATTN_QB, nblk - 1), c)

    return pl.pallas_call(
        functools.partial(_attn_body, seq_len),
        grid=(t // ATTN_TQ,),
        in_specs=[
            pl.BlockSpec((ATTN_TQ, ATTN_WIDTH), lambda i: (i, 0)),
            pl.BlockSpec((BLOCK, KV_WIDTH), prev_map(kcol)),
            pl.BlockSpec((ATTN_TQ, KV_WIDTH), lambda i: (i, kcol)),
            pl.BlockSpec((BLOCK, KV_WIDTH), next_map(kcol)),
            pl.BlockSpec((BLOCK, KV_WIDTH), prev_map(vcol)),
            pl.BlockSpec((ATTN_TQ, KV_WIDTH), lambda i: (i, vcol)),
            pl.BlockSpec((BLOCK, KV_WIDTH), next_map(vcol)),
            pl.BlockSpec((N_HEADS, 3 * BLOCK, BLOCK), lambda i: (0, 0, 0)),
            pl.BlockSpec(memory_space=pltpu.SMEM),
        ],
        out_specs=pl.BlockSpec((ATTN_TQ, ATTN_WIDTH), lambda i: (i, 0)),
        out_shape=jax.ShapeDtypeStruct((t, ATTN_WIDTH), BF16),
        compiler_params=_params(("parallel",)),
        name="attn",
    )(qkv, qkv, qkv, qkv, qkv, qkv, qkv, bias, sink)


CONV_TS = 512
CONV_RC = 64
CONV_NC = CONV_WIDTH // LANES


def _conv_body(seq_len, up_ref, um_ref, un_ref, w_ref, cb_ref, g_ref, b_ref, o_ref, buf_ref, h_ref):
    i = pl.program_id(0)
    t0 = i * CONV_TS
    first = lax.rem(t0, seq_len) == 0
    last = lax.rem(t0 + CONV_TS, seq_len) == 0
    shift = CONV_HALO - CONV_K // 2
    groups = CONV_RC // SUBLANES

    for c in range(CONV_NC):
        cs = slice(c * LANES, (c + 1) * LANES)
        buf_ref[c, 0:CONV_HALO, :] = jnp.where(first, 0.0, up_ref[:, cs])
        buf_ref[c, CONV_HALO:CONV_HALO + CONV_TS, :] = um_ref[:, cs]
        buf_ref[c, CONV_HALO + CONV_TS:, :] = jnp.where(last, 0.0, un_ref[:, cs])

        def rows(r, carry, c=c, cs=cs):
            r0 = pl.multiple_of(r * CONV_RC, CONV_RC)
            win = [buf_ref[c, pl.ds(r0 + shift + off, SUBLANES), :]
                   for off in range(CONV_K + CONV_RC - SUBLANES)]
            accs = [cb_ref[:, cs]] * groups
            for k in range(CONV_K):
                wk = w_ref[k * SUBLANES:(k + 1) * SUBLANES, cs]
                accs = [accs[j] + win[k + SUBLANES * j] * wk for j in range(groups)]
            for j in range(groups):
                h_ref[pl.ds(r0 + SUBLANES * j, SUBLANES), cs] = accs[j]
            return carry

        lax.fori_loop(0, CONV_TS // CONV_RC, rows, 0)

    def store(r0, y):
        o_ref[pl.ds(r0, LN_ROWS), :] = (y * jax.nn.sigmoid(y)).astype(BF16)

    _ln_loop(CONV_TS, lambda r0: h_ref[pl.ds(r0, LN_ROWS), :], store, g_ref, b_ref)


def _conv_call(u, conv_w, conv_b, ln_g, ln_b, seq_len):
    t = u.shape[0]
    per = CONV_TS // CONV_HALO
    nh = t // CONV_HALO
    w_rep = jnp.repeat(conv_w, SUBLANES, axis=0)
    cb_rep = jnp.broadcast_to(conv_b[None, :], (SUBLANES, CONV_WIDTH))
    full = lambda a: pl.BlockSpec(a.shape, lambda i: (0, 0))
    g2, b2 = ln_g.reshape(1, CONV_WIDTH), ln_b.reshape(1, CONV_WIDTH)
    return pl.pallas_call(
        functools.partial(_conv_body, seq_len),
        grid=(t // CONV_TS,),
        in_specs=[
            pl.BlockSpec((CONV_HALO, CONV_WIDTH), lambda i: (jnp.maximum(i * per - 1, 0), 0)),
            pl.BlockSpec((CONV_TS, CONV_WIDTH), lambda i: (i, 0)),
            pl.BlockSpec((CONV_HALO, CONV_WIDTH), lambda i: (jnp.minimum((i + 1) * per, nh - 1), 0)),
            full(w_rep), full(cb_rep), full(g2), full(b2),
        ],
        out_specs=pl.BlockSpec((CONV_TS, CONV_WIDTH), lambda i: (i, 0)),
        out_shape=jax.ShapeDtypeStruct((t, CONV_WIDTH), BF16),
        scratch_shapes=[
            pltpu.VMEM((CONV_NC, CONV_TS + 2 * CONV_HALO, LANES), F32),
            pltpu.VMEM((CONV_TS, CONV_WIDTH), F32),
        ],
        compiler_params=_params(("parallel",)),
        name="conv",
    )(u, u, u, w_rep, cb_rep, g2, b2)


def _mixout_body(c_ref, a_ref, wc_ref, wa_ref, x_ref, o_ref):
    mix = _dot(c_ref[...], wc_ref[...]) + _dot(a_ref[...], wa_ref[...])
    o_ref[...] = ALPHA * x_ref[...] + mix


MIX_BN = 1024


def _mixout_call(conv_out, attn_out, w_out, x):
    t = x.shape[0]
    return pl.pallas_call(
        _mixout_body,
        grid=(t // PROJ_BM, D_MODEL // MIX_BN),
        in_specs=[
            pl.BlockSpec((PROJ_BM, CONV_WIDTH), lambda i, j: (i, 0)),
            pl.BlockSpec((PROJ_BM, ATTN_WIDTH), lambda i, j: (i, 0)),
            pl.BlockSpec((CONV_WIDTH, MIX_BN), lambda i, j: (0, j)),
            pl.BlockSpec((ATTN_WIDTH, MIX_BN), lambda i, j: (1, j)),
            pl.BlockSpec((PROJ_BM, MIX_BN), lambda i, j: (i, j)),
        ],
        out_specs=pl.BlockSpec((PROJ_BM, MIX_BN), lambda i, j: (i, j)),
        out_shape=jax.ShapeDtypeStruct((t, D_MODEL), F32),
        compiler_params=_params(("parallel", "arbitrary")),
        name="mixout",
    )(conv_out, attn_out, w_out, w_out, x)


XA_BM = 256


def _xattn_body(z_ref, g1_ref, b1_ref, wq_ref, k_ref, v_ref, wo_ref, g2_ref, b2_ref, o_ref, ob_ref,
                x1_ref, x1b_ref, z2_ref):
    def store_x1(r0, x1):
        x1_ref[pl.ds(r0, LN_ROWS), :] = x1
        x1b_ref[pl.ds(r0, LN_ROWS), :] = x1.astype(BF16)

    _ln_loop(XA_BM, lambda r0: z_ref[pl.ds(r0, LN_ROWS), :], store_x1, g1_ref, b1_ref)

    q = _dot(x1b_ref[...], wq_ref[...])
    outs = []
    for h in range(X_HEADS):
        cs = slice(h * HEAD_DIM, (h + 1) * HEAD_DIM)
        s = _dot_nt(q[:, cs].astype(BF16), k_ref[0, :, cs]) * SCALE
        m = jnp.max(s, axis=-1, keepdims=True)
        e = jnp.exp(s - m)
        p = (e / jnp.sum(e, axis=-1, keepdims=True)).astype(BF16)
        outs.append(_dot(p, v_ref[0, :, cs]))
    o = jnp.concatenate(outs, axis=-1).astype(BF16)
    z2_ref[...] = ALPHA * x1_ref[...] + _dot(o, wo_ref[...])

    def store_x2(r0, x2):
        o_ref[pl.ds(r0, LN_ROWS), :] = x2
        ob_ref[pl.ds(r0, LN_ROWS), :] = x2.astype(BF16)

    _ln_loop(XA_BM, lambda r0: z2_ref[pl.ds(r0, LN_ROWS), :], store_x2, g2_ref, b2_ref)


def _xattn_call(z1, ln1_g, ln1_b, xq_w, kx, vx, xo_w, ln2_g, ln2_b, seq_len):
    t = z1.shape[0]
    per = seq_len // XA_BM
    vec = pl.BlockSpec((1, D_MODEL), lambda i: (0, 0))
    return pl.pallas_call(
        _xattn_body,
        grid=(t // XA_BM,),
        in_specs=[
            pl.BlockSpec((XA_BM, D_MODEL), lambda i: (i, 0)),
            vec, vec,
            pl.BlockSpec((D_MODEL, X_WIDTH), lambda i: (0, 0)),
            pl.BlockSpec((1, MEM_LEN, X_WIDTH), lambda i: (i // per, 0, 0)),
            pl.BlockSpec((1, MEM_LEN, X_WIDTH), lambda i: (i // per, 0, 0)),
            pl.BlockSpec((X_WIDTH, D_MODEL), lambda i: (0, 0)),
            vec, vec,
        ],
        out_specs=[
            pl.BlockSpec((XA_BM, D_MODEL), lambda i: (i, 0)),
            pl.BlockSpec((XA_BM, D_MODEL), lambda i: (i, 0)),
        ],
        out_shape=[
            jax.ShapeDtypeStruct((t, D_MODEL), F32),
            jax.ShapeDtypeStruct((t, D_MODEL), BF16),
        ],
        scratch_shapes=[
            pltpu.VMEM((XA_BM, D_MODEL), F32),
            pltpu.VMEM((XA_BM, D_MODEL), BF16),
            pltpu.VMEM((XA_BM, D_MODEL), F32),
        ],
        compiler_params=_params(("parallel",)),
        name="xattn",
    )(z1, ln1_g.reshape(1, D_MODEL), ln1_b.reshape(1, D_MODEL), xq_w, kx, vx, xo_w,
      ln2_g.reshape(1, D_MODEL), ln2_b.reshape(1, D_MODEL))


FFN_BM = 1024
FFN_TF = 256
FFN_NJ = D_FF // FFN_TF
FFN_EP = FFN_BM // LN_ROWS
FFN_ZERO_ROWS = 64


def _ffn_body(xb_ref, wg_ref, wu_ref, wd_ref, g_ref, b_ref, x_hbm, o_hbm, acc_ref, xres_ref, xsem, osem):
    i = pl.program_id(0)
    j = pl.program_id(1)
    row0 = i * FFN_BM

    def x_copy(c):
        return pltpu.make_async_copy(
            x_hbm.at[pl.ds(row0 + c * LN_ROWS, LN_ROWS), :], xres_ref.at[c % 2], xsem.at[c % 2])

    def o_copy(c):
        return pltpu.make_async_copy(
            acc_ref.at[pl.ds(c * LN_ROWS, LN_ROWS), :],
            o_hbm.at[pl.ds(row0 + c * LN_ROWS, LN_ROWS), :], osem.at[c])

    @pl.when(j == 0)
    def _():
        def rows(r, carry):
            r0 = pl.multiple_of(r * FFN_ZERO_ROWS, FFN_ZERO_ROWS)
            acc_ref[pl.ds(r0, FFN_ZERO_ROWS), :] = jnp.zeros((FFN_ZERO_ROWS, D_MODEL), F32)
            return carry

        lax.fori_loop(0, FFN_BM // FFN_ZERO_ROWS, rows, 0)

    @pl.when(j == FFN_NJ - 1)
    def _():
        x_copy(0).start()
        x_copy(1).start()

    xb = xb_ref[...]
    gate = _dot(xb, wg_ref[...])
    up = _dot(xb, wu_ref[...])
    a = (gate * jax.nn.sigmoid(gate) * up).astype(BF16)
    acc_ref[...] += _dot(a, wd_ref[...])

    @pl.when(j == FFN_NJ - 1)
    def _():
        for c in range(FFN_EP):
            rs = slice(c * LN_ROWS, (c + 1) * LN_ROWS)
            x_copy(c).wait()
            z = ALPHA * xres_ref[c % 2] + acc_ref[rs, :]
            acc_ref[rs, :] = _ln_rows(z, g_ref[...], b_ref[...])
            if c + 2 < FFN_EP:
                x_copy(c + 2).start()
            o_copy(c).start()
        for c in range(FFN_EP):
            o_copy(c).wait()


def _ffn_call(x2, x2b, w_gate, w_up, w_down, ln_g, ln_b):
    t = x2.shape[0]
    return pl.pallas_call(
        _ffn_body,
        grid=(t // FFN_BM, FFN_NJ),
        in_specs=[
            pl.BlockSpec((FFN_BM, D_MODEL), lambda i, j: (i, 0)),
            pl.BlockSpec((D_MODEL, FFN_TF), lambda i, j: (0, j)),
            pl.BlockSpec((D_MODEL, FFN_TF), lambda i, j: (0, j)),
            pl.BlockSpec((FFN_TF, D_MODEL), lambda i, j: (j, 0)),
            pl.BlockSpec((1, D_MODEL), lambda i, j: (0, 0)),
            pl.BlockSpec((1, D_MODEL), lambda i, j: (0, 0)),
            pl.BlockSpec(memory_space=pl.ANY),
        ],
        out_specs=pl.BlockSpec(memory_space=pl.ANY),
        out_shape=jax.ShapeDtypeStruct((t, D_MODEL), F32),
        scratch_shapes=[
            pltpu.VMEM((FFN_BM, D_MODEL), F32),
            pltpu.VMEM((2, LN_ROWS, D_MODEL), F32),
            pltpu.SemaphoreType.DMA((2,)),
            pltpu.SemaphoreType.DMA((FFN_EP,)),
        ],
        compiler_params=_params(("arbitrary", "arbitrary")),
        name="ffn",
    )(x2b, w_gate, w_up, w_down, ln_g.reshape(1, D_MODEL), ln_b.reshape(1, D_MODEL), x2)


def _layer(x, kx, vx, bias, w, seq_len):
    qkv, xb = _qkv_call(x, w["w_in"])
    u = _glu_call(xb, w["w_in"])
    attn_out = _attn_call(qkv, bias, w["sink"], seq_len)
    conv_out = _conv_call(u, w["conv_w"], w["conv_b"], w["conv_ln_g"], w["conv_ln_b"], seq_len)
    z1 = _mixout_call(conv_out, attn_out, w["w_out"], x)
    x2, x2b = _xattn_call(z1, w["ln1_g"], w["ln1_b"], w["xq_w"], kx, vx, w["xo_w"], w["ln2_g"],
                          w["ln2_b"], seq_len)
    return _ffn_call(x2, x2b, w["w_gate"], w["w_up"], w["w_down"], w["ln3_g"], w["ln3_b"])


def kernel(x_prompt, x_sample, mem_prompt, mem_sample, rel_bias, w_in, conv_w, conv_b, conv_ln_g,
           conv_ln_b, sink, w_out, ln1_g, ln1_b, xq_w, xk_w, xv_w, xo_w, ln2_g, ln2_b, w_gate, w_up,
           w_down, ln3_g, ln3_b):
    bp, sp, _ = x_prompt.shape
    bs, ss, _ = x_sample.shape
    xs = [x_prompt.reshape(bp * sp, D_MODEL), x_sample.reshape(bs * ss, D_MODEL)]
    mems = [mem_prompt, mem_sample]
    bias = _bias_call(rel_bias)
    for l in range(DEPTH):
        w = {
            "w_in": w_in[l].astype(BF16), "conv_w": conv_w[l], "conv_b": conv_b[l],
            "conv_ln_g": conv_ln_g[l], "conv_ln_b": conv_ln_b[l], "sink": sink[l],
            "w_out": w_out[l].astype(BF16), "ln1_g": ln1_g[l], "ln1_b": ln1_b[l],
            "xq_w": xq_w[l].astype(BF16), "xo_w": xo_w[l].astype(BF16),
            "ln2_g": ln2_g[l], "ln2_b": ln2_b[l],
            "w_gate": w_gate[l].astype(BF16), "w_up": w_up[l].astype(BF16),
            "w_down": w_down[l].astype(BF16), "ln3_g": ln3_g[l], "ln3_b": ln3_b[l],
        }
        wkv = jnp.concatenate([xk_w[l], xv_w[l]], axis=-1).astype(BF16)
        memb = jnp.concatenate([m.reshape(-1, D_MODEL) for m in mems], axis=0).astype(BF16)
        kv = _mm_call(memb, wkv, 0, 2 * X_WIDTH, MEM_LEN, X_WIDTH, "memkv")
        kv = kv.reshape(bp + bs, MEM_LEN, 2 * X_WIDTH)
        kx, vx = kv[..., :X_WIDTH], kv[..., X_WIDTH:]
        xs = [
            _layer(xs[0], kx[:bp], vx[:bp], bias, w, sp),
            _layer(xs[1], kx[bp:], vx[bp:], bias, w, ss),
        ]
    return (xs[0].reshape(bp, sp, D_MODEL), xs[1].reshape(bs, ss, D_MODEL))
```

```python
import functools
import math

import jax
import jax.numpy as jnp
from jax import lax
from jax.experimental import pallas as pl
from jax.experimental.pallas import tpu as pltpu

D_MODEL = 4096
HEAD_DIM = 128
CONV_WIDTH = 2048
ATTN_WIDTH = 2048
N_HEADS = 16
N_KV_HEADS = 4
GROUP = 4
KV_WIDTH = 512
CONV_K = 31
WINDOW = 128
BLOCK = 128
N_BUCKETS = 32
MAX_DISTANCE = 128
MEM_LEN = 256
X_HEADS = 4
X_WIDTH = 512
D_FF = 11008
DEPTH = 1
ALPHA = (2 * DEPTH) ** 0.25
LN_EPS = 1e-5
NEG_INF = -1e30
SCALE = HEAD_DIM ** -0.5
LOG2E = math.log2(math.e)

V7X_VMEM_BYTES = 64 * 1024 * 1024
VMEM_LIMIT = V7X_VMEM_BYTES - 8 * 1024 * 1024
LANES = 128
SUBLANES = 8
CONV_HALO = 16

BF16 = jnp.bfloat16
F32 = jnp.float32


def _params(sem):
    return pltpu.CompilerParams(dimension_semantics=sem, vmem_limit_bytes=VMEM_LIMIT)


def _dot(a, b):
    return jnp.dot(a, b, preferred_element_type=F32)


def _dot_nt(a, b):
    return lax.dot_general(a, b, (((1,), (1,)), ((), ())), preferred_element_type=F32)


LN_ROWS = 128


def _ln_rows(z, g, b):
    mu = jnp.mean(z, axis=-1, keepdims=True)
    zc = z - mu
    var = jnp.mean(zc * zc, axis=-1, keepdims=True)
    return zc * lax.rsqrt(var + LN_EPS) * g + b


def _ln_loop(nrows, load, store, g_ref, b_ref):
    def rows(r, carry):
        r0 = pl.multiple_of(r * LN_ROWS, LN_ROWS)
        store(r0, _ln_rows(load(r0), g_ref[...], b_ref[...]))
        return carry

    lax.fori_loop(0, nrows // LN_ROWS, rows, 0)


PROJ_BM = 1024
PROJ_BN = 512
PROJ_ROWS = 256


CAST_SLAB = 256
CAST_SLAB_ROWS = 128


def _cast_job(w, grid):
    nblk = w.shape[1] // CAST_SLAB
    assert w.shape[1] % CAST_SLAB == 0 and math.prod(grid) >= nblk
    spec = pl.BlockSpec((w.shape[0], CAST_SLAB), lambda i, j: (0, jnp.minimum(i * grid[1] + j, nblk - 1)))
    return spec, jax.ShapeDtypeStruct(w.shape, BF16)


def _cast_slab(nblk, w_ref, o_ref):
    step = pl.program_id(0) * pl.num_programs(1) + pl.program_id(1)

    @pl.when(step < nblk)
    def _():
        def rows(r, carry):
            r0 = pl.multiple_of(r * CAST_SLAB_ROWS, CAST_SLAB_ROWS)
            o_ref[pl.ds(r0, CAST_SLAB_ROWS), :] = w_ref[pl.ds(r0, CAST_SLAB_ROWS), :].astype(BF16)
            return carry

        lax.fori_loop(0, w_ref.shape[0] // CAST_SLAB_ROWS, rows, 0)


def _glu_body(nblk, x_ref, wv_ref, wg_ref, wf_ref, o_ref, wb_ref):
    _cast_slab(nblk, wf_ref, wb_ref)
    for r0 in range(0, x_ref.shape[0], PROJ_ROWS):
        x = x_ref[r0:r0 + PROJ_ROWS, :]
        v = _dot(x, wv_ref[...])
        g = _dot(x, wg_ref[...])
        o_ref[r0:r0 + PROJ_ROWS, :] = v * jax.nn.sigmoid(g)


def _glu_call(xb, w_in, w_cast):
    t = xb.shape[0]
    nb = CONV_WIDTH // PROJ_BN
    grid = (t // PROJ_BM, nb)
    cast_spec, cast_shape = _cast_job(w_cast, grid)
    return pl.pallas_call(
        functools.partial(_glu_body, w_cast.shape[1] // CAST_SLAB),
        grid=grid,
        in_specs=[
            pl.BlockSpec((PROJ_BM, D_MODEL), lambda i, j: (i, 0)),
            pl.BlockSpec((D_MODEL, PROJ_BN), lambda i, j: (0, j)),
            pl.BlockSpec((D_MODEL, PROJ_BN), lambda i, j: (0, j + nb)),
            cast_spec,
        ],
        out_specs=[pl.BlockSpec((PROJ_BM, PROJ_BN), lambda i, j: (i, j)), cast_spec],
        out_shape=[jax.ShapeDtypeStruct((t, CONV_WIDTH), F32), cast_shape],
        compiler_params=_params(("arbitrary", "arbitrary")),
        name="glu",
    )(xb, w_in, w_in, w_cast)


QKV_BM = 512
QKV_BN = 1024
CAST_ROWS = 32


def _qkv_body(x_ref, w_ref, o_ref, xb_ref):
    @pl.when(pl.program_id(1) == 0)
    def _():
        def rows(r, carry):
            r0 = pl.multiple_of(r * CAST_ROWS, CAST_ROWS)
            xb_ref[pl.ds(r0, CAST_ROWS), :] = x_ref[pl.ds(r0, CAST_ROWS), :].astype(BF16)
            return carry

        lax.fori_loop(0, QKV_BM // CAST_ROWS, rows, 0)

    o_ref[...] = _dot(xb_ref[...], w_ref[...]).astype(BF16)


def _qkv_call(x, w_in):
    t = x.shape[0]
    n = ATTN_WIDTH + 2 * KV_WIDTH
    off = 2 * CONV_WIDTH // QKV_BN
    return pl.pallas_call(
        _qkv_body,
        grid=(t // QKV_BM, n // QKV_BN),
        in_specs=[
            pl.BlockSpec((QKV_BM, D_MODEL), lambda i, j: (i, 0)),
            pl.BlockSpec((D_MODEL, QKV_BN), lambda i, j: (0, j + off)),
        ],
        out_specs=[
            pl.BlockSpec((QKV_BM, QKV_BN), lambda i, j: (i, j)),
            pl.BlockSpec((QKV_BM, D_MODEL), lambda i, j: (i, 0)),
        ],
        out_shape=[
            jax.ShapeDtypeStruct((t, n), BF16),
            jax.ShapeDtypeStruct((t, D_MODEL), BF16),
        ],
        compiler_params=_params(("parallel", "arbitrary")),
        name="qkv",
    )(x, w_in)


def _mm_body(x_ref, w_ref, o_ref):
    o_ref[...] = _dot(x_ref[...], w_ref[...]).astype(o_ref.dtype)


def _mm_call(xb, w, col_off, n, bm, bn, name):
    t, k = xb.shape
    off = col_off // bn
    return pl.pallas_call(
        _mm_body,
        grid=(t // bm, n // bn),
        in_specs=[
            pl.BlockSpec((bm, k), lambda i, j: (i, 0)),
            pl.BlockSpec((k, bn), lambda i, j: (0, j + off)),
        ],
        out_specs=pl.BlockSpec((bm, bn), lambda i, j: (i, j)),
        out_shape=jax.ShapeDtypeStruct((t, n), BF16),
        compiler_params=_params(("parallel", "arbitrary")),
        name=name,
    )(xb, w)


def _bias_body(bucket_ref, rb_ref, o_ref):
    h = pl.program_id(0)
    bkt = bucket_ref[...]
    acc = jnp.zeros(bkt.shape, F32)
    for b in range(N_BUCKETS):
        acc = jnp.where(bkt == b, rb_ref[b, h], acc)
    o_ref[0] = acc * LOG2E


def _t5_bucket(rel):
    nb = N_BUCKETS // 2
    max_exact = nb // 2
    ret = jnp.where(rel > 0, nb, 0)
    n = jnp.abs(rel)
    nf = jnp.maximum(n, 1).astype(F32)
    large = max_exact + (jnp.log(nf / max_exact) / math.log(MAX_DISTANCE / max_exact)
                         * (nb - max_exact)).astype(jnp.int32)
    large = jnp.minimum(large, nb - 1)
    return ret + jnp.where(n < max_exact, n, large)


def _bias_call(rel_bias):
    rel = jnp.arange(3 * BLOCK)[:, None] - BLOCK - jnp.arange(BLOCK)[None, :]
    bucket = _t5_bucket(rel).astype(jnp.int32)
    return pl.pallas_call(
        _bias_body,
        grid=(N_HEADS,),
        in_specs=[
            pl.BlockSpec((3 * BLOCK, BLOCK), lambda h: (0, 0)),
            pl.BlockSpec(memory_space=pltpu.SMEM),
        ],
        out_specs=pl.BlockSpec((1, 3 * BLOCK, BLOCK), lambda h: (h, 0, 0)),
        out_shape=jax.ShapeDtypeStruct((N_HEADS, 3 * BLOCK, BLOCK), F32),
        compiler_params=_params(("arbitrary",)),
        name="t5bias",
    )(bucket, rel_bias)


ATTN_TQ = 512
ATTN_QB = ATTN_TQ // BLOCK


def _attn_body(seq_len, q_ref, kp_ref, km_ref, kn_ref, vp_ref, vm_ref, vn_ref, bias_ref, sink_ref,
               o_ref):
    i = pl.program_id(0)
    t0 = i * ATTN_TQ
    key = lax.broadcasted_iota(jnp.int32, (3 * BLOCK, GROUP * BLOCK), 0)
    qry = lax.broadcasted_iota(jnp.int32, (3 * BLOCK, GROUP * BLOCK), 1) & (BLOCK - 1)
    band = (key >= qry) & (key <= qry + 2 * WINDOW)
    for kh in range(N_KV_HEADS):
        cs = slice(kh * HEAD_DIM, (kh + 1) * HEAD_DIM)
        kall = jnp.concatenate([kp_ref[:, cs], km_ref[:, cs], kn_ref[:, cs]], axis=0)
        vt = jnp.concatenate([vp_ref[:, cs], vm_ref[:, cs], vn_ref[:, cs]], axis=0).T
        bias = jnp.concatenate([bias_ref[kh * GROUP + g] for g in range(GROUP)], axis=1)
        sk = jnp.concatenate(
            [jnp.full((1, BLOCK), sink_ref[kh * GROUP + g] * LOG2E, F32) for g in range(GROUP)], axis=1)
        for qb in range(ATTN_QB):
            rs = slice(qb * BLOCK, (qb + 1) * BLOCK)
            pos = t0 + qb * BLOCK
            first = lax.rem(pos, seq_len) == 0
            last = lax.rem(pos + BLOCK, seq_len) == 0
            lo = jnp.where(first, BLOCK, 0)
            hi = jnp.where(last, 2 * BLOCK, 3 * BLOCK)
            valid = band & (key >= lo) & (key < hi)
            kb = kall[qb * BLOCK:(qb + 3) * BLOCK]
            vb = vt[:, qb * BLOCK:(qb + 3) * BLOCK]
            for g in range(GROUP):
                h = kh * GROUP + g
                ls = slice(g * BLOCK, (g + 1) * BLOCK)
                q = q_ref[rs, h * HEAD_DIM:(h + 1) * HEAD_DIM]
                s = _dot_nt(kb, q) * (SCALE * LOG2E)
                s = jnp.where(valid[:, ls], s + bias[:, ls], NEG_INF)
                m = jnp.maximum(jnp.max(s, axis=0, keepdims=True), sk[:, ls])
                p = jnp.exp2(s - m)
                denom = jnp.sum(p, axis=0, keepdims=True) + jnp.exp2(sk[:, ls] - m)
                p = (p * (1.0 / denom)).astype(BF16)
                ot = _dot(vb, p)
                o_ref[rs, h * HEAD_DIM:(h + 1) * HEAD_DIM] = ot.T.astype(BF16)


def _attn_call(qkv, bias, sink, seq_len):
    t = qkv.shape[0]
    nblk = t // BLOCK
    kcol = ATTN_WIDTH // KV_WIDTH
    vcol = kcol + 1

    def prev_map(c):
        return lambda i: (jnp.maximum(i * ATTN_QB - 1, 0), c)

    def next_map(c):
        return lambda i: (jnp.minimum((i + 1) * ATTN_QB, nblk - 1), c)

    return pl.pallas_call(
        functools.partial(_attn_body, seq_len),
        grid=(t // ATTN_TQ,),
        in_specs=[
            pl.BlockSpec((ATTN_TQ, ATTN_WIDTH), lambda i: (i, 0)),
            pl.BlockSpec((BLOCK, KV_WIDTH), prev_map(kcol)),
            pl.BlockSpec((ATTN_TQ, KV_WIDTH), lambda i: (i, kcol)),
            pl.BlockSpec((BLOCK, KV_WIDTH), next_map(kcol)),
            pl.BlockSpec((BLOCK, KV_WIDTH), prev_map(vcol)),
            pl.BlockSpec((ATTN_TQ, KV_WIDTH), lambda i: (i, vcol)),
            pl.BlockSpec((BLOCK, KV_WIDTH), next_map(vcol)),
            pl.BlockSpec((N_HEADS, 3 * BLOCK, BLOCK), lambda i: (0, 0, 0)),
            pl.BlockSpec(memory_space=pltpu.SMEM),
        ],
        out_specs=pl.BlockSpec((ATTN_TQ, ATTN_WIDTH), lambda i: (i, 0)),
        out_shape=jax.ShapeDtypeStruct((t, ATTN_WIDTH), BF16),
        compiler_params=_params(("parallel",)),
        name="attn",
    )(qkv, qkv, qkv, qkv, qkv, qkv, qkv, bias, sink)


CONV_TS = 512
CONV_RC = 64
CONV_NC = CONV_WIDTH // LANES


def _conv_body(seq_len, up_ref, um_ref, un_ref, w_ref, cb_ref, g_ref, b_ref, o_ref, buf_ref, h_ref):
    i = pl.program_id(0)
    t0 = i * CONV_TS
    first = lax.rem(t0, seq_len) == 0
    last = lax.rem(t0 + CONV_TS, seq_len) == 0
    shift = CONV_HALO - CONV_K // 2
    groups = CONV_RC // SUBLANES

    for c in range(CONV_NC):
        cs = slice(c * LANES, (c + 1) * LANES)
        buf_ref[c, 0:CONV_HALO, :] = jnp.where(first, 0.0, up_ref[:, cs])
        buf_ref[c, CONV_HALO:CONV_HALO + CONV_TS, :] = um_ref[:, cs]
        buf_ref[c, CONV_HALO + CONV_TS:, :] = jnp.where(last, 0.0, un_ref[:, cs])

        def rows(r, carry, c=c, cs=cs):
            r0 = pl.multiple_of(r * CONV_RC, CONV_RC)
            win = [buf_ref[c, pl.ds(r0 + shift + off, SUBLANES), :]
                   for off in range(CONV_K + CONV_RC - SUBLANES)]
            accs = [cb_ref[:, cs]] * groups
            for k in range(CONV_K):
                wk = w_ref[k * SUBLANES:(k + 1) * SUBLANES, cs]
                accs = [accs[j] + win[k + SUBLANES * j] * wk for j in range(groups)]
            for j in range(groups):
                h_ref[pl.ds(r0 + SUBLANES * j, SUBLANES), cs] = accs[j]
            return carry

        lax.fori_loop(0, CONV_TS // CONV_RC, rows, 0)

    def store(r0, y):
        o_ref[pl.ds(r0, LN_ROWS), :] = (y * jax.nn.sigmoid(y)).astype(BF16)

    _ln_loop(CONV_TS, lambda r0: h_ref[pl.ds(r0, LN_ROWS), :], store, g_ref, b_ref)


def _conv_call(u, conv_w, conv_b, ln_g, ln_b, seq_len):
    t = u.shape[0]
    per = CONV_TS // CONV_HALO
    nh = t // CONV_HALO
    w_rep = jnp.repeat(conv_w, SUBLANES, axis=0)
    cb_rep = jnp.broadcast_to(conv_b[None, :], (SUBLANES, CONV_WIDTH))
    full = lambda a: pl.BlockSpec(a.shape, lambda i: (0, 0))
    g2, b2 = ln_g.reshape(1, CONV_WIDTH), ln_b.reshape(1, CONV_WIDTH)
    return pl.pallas_call(
        functools.partial(_conv_body, seq_len),
        grid=(t // CONV_TS,),
        in_specs=[
            pl.BlockSpec((CONV_HALO, CONV_WIDTH), lambda i: (jnp.maximum(i * per - 1, 0), 0)),
            pl.BlockSpec((CONV_TS, CONV_WIDTH), lambda i: (i, 0)),
            pl.BlockSpec((CONV_HALO, CONV_WIDTH), lambda i: (jnp.minimum((i + 1) * per, nh - 1), 0)),
            full(w_rep), full(cb_rep), full(g2), full(b2),
        ],
        out_specs=pl.BlockSpec((CONV_TS, CONV_WIDTH), lambda i: (i, 0)),
        out_shape=jax.ShapeDtypeStruct((t, CONV_WIDTH), BF16),
        scratch_shapes=[
            pltpu.VMEM((CONV_NC, CONV_TS + 2 * CONV_HALO, LANES), F32),
            pltpu.VMEM((CONV_TS, CONV_WIDTH), F32),
        ],
        compiler_params=_params(("parallel",)),
        name="conv",
    )(u, u, u, w_rep, cb_rep, g2, b2)


def _mixout_body(c_ref, a_ref, wc_ref, wa_ref, x_ref, o_ref):
    mix = _dot(c_ref[...], wc_ref[...]) + _dot(a_ref[...], wa_ref[...])
    o_ref[...] = ALPHA * x_ref[...] + mix


MIX_BN = 1024


def _mixout_call(conv_out, attn_out, w_out, x):
    t = x.shape[0]
    return pl.pallas_call(
        _mixout_body,
        grid=(t // PROJ_BM, D_MODEL // MIX_BN),
        in_specs=[
            pl.BlockSpec((PROJ_BM, CONV_WIDTH), lambda i, j: (i, 0)),
            pl.BlockSpec((PROJ_BM, ATTN_WIDTH), lambda i, j: (i, 0)),
            pl.BlockSpec((CONV_WIDTH, MIX_BN), lambda i, j: (0, j)),
            pl.BlockSpec((ATTN_WIDTH, MIX_BN), lambda i, j: (1, j)),
            pl.BlockSpec((PROJ_BM, MIX_BN), lambda i, j: (i, j)),
        ],
        out_specs=pl.BlockSpec((PROJ_BM, MIX_BN), lambda i, j: (i, j)),
        out_shape=jax.ShapeDtypeStruct((t, D_MODEL), F32),
        compiler_params=_params(("parallel", "arbitrary")),
        name="mixout",
    )(conv_out, attn_out, w_out, w_out, x)


XA_BM = 256


def _xattn_body(z_ref, g1_ref, b1_ref, wq_ref, k_ref, v_ref, wo_ref, g2_ref, b2_ref, o_ref, ob_ref,
                x1_ref, x1b_ref, z2_ref):
    def store_x1(r0, x1):
        x1_ref[pl.ds(r0, LN_ROWS), :] = x1
        x1b_ref[pl.ds(r0, LN_ROWS), :] = x1.astype(BF16)

    _ln_loop(XA_BM, lambda r0: z_ref[pl.ds(r0, LN_ROWS), :], store_x1, g1_ref, b1_ref)

    q = _dot(x1b_ref[...], wq_ref[...])
    outs = []
    for h in range(X_HEADS):
        cs = slice(h * HEAD_DIM, (h + 1) * HEAD_DIM)
        s = _dot_nt(q[:, cs].astype(BF16), k_ref[0, :, cs]) * SCALE
        m = jnp.max(s, axis=-1, keepdims=True)
        e = jnp.exp(s - m)
        p = (e / jnp.sum(e, axis=-1, keepdims=True)).astype(BF16)
        outs.append(_dot(p, v_ref[0, :, cs]))
    o = jnp.concatenate(outs, axis=-1).astype(BF16)
    z2_ref[...] = ALPHA * x1_ref[...] + _dot(o, wo_ref[...])

    def store_x2(r0, x2):
        o_ref[pl.ds(r0, LN_ROWS), :] = x2
        ob_ref[pl.ds(r0, LN_ROWS), :] = x2.astype(BF16)

    _ln_loop(XA_BM, lambda r0: z2_ref[pl.ds(r0, LN_ROWS), :], store_x2, g2_ref, b2_ref)


def _xattn_call(z1, ln1_g, ln1_b, xq_w, kx, vx, xo_w, ln2_g, ln2_b, seq_len):
    t = z1.shape[0]
    per = seq_len // XA_BM
    vec = pl.BlockSpec((1, D_MODEL), lambda i: (0, 0))
    return pl.pallas_call(
        _xattn_body,
        grid=(t // XA_BM,),
        in_specs=[
            pl.BlockSpec((XA_BM, D_MODEL), lambda i: (i, 0)),
            vec, vec,
            pl.BlockSpec((D_MODEL, X_WIDTH), lambda i: (0, 0)),
            pl.BlockSpec((1, MEM_LEN, X_WIDTH), lambda i: (i // per, 0, 0)),
            pl.BlockSpec((1, MEM_LEN, X_WIDTH), lambda i: (i // per, 0, 0)),
            pl.BlockSpec((X_WIDTH, D_MODEL), lambda i: (0, 0)),
            vec, vec,
        ],
        out_specs=[
            pl.BlockSpec((XA_BM, D_MODEL), lambda i: (i, 0)),
            pl.BlockSpec((XA_BM, D_MODEL), lambda i: (i, 0)),
        ],
        out_shape=[
            jax.ShapeDtypeStruct((t, D_MODEL), F32),
            jax.ShapeDtypeStruct((t, D_MODEL), BF16),
        ],
        scratch_shapes=[
            pltpu.VMEM((XA_BM, D_MODEL), F32),
            pltpu.VMEM((XA_BM, D_MODEL), BF16),
            pltpu.VMEM((XA_BM, D_MODEL), F32),
        ],
        compiler_params=_params(("parallel",)),
        name="xattn",
    )(z1, ln1_g.reshape(1, D_MODEL), ln1_b.reshape(1, D_MODEL), xq_w, kx, vx, xo_w,
      ln2_g.reshape(1, D_MODEL), ln2_b.reshape(1, D_MODEL))


FFN_BM = 1024
FFN_TF = 256
FFN_NJ = D_FF // FFN_TF
FFN_EP = FFN_BM // LN_ROWS
FFN_ZERO_ROWS = 64


def _ffn_body(xb_ref, wg_ref, wu_ref, wd_ref, g_ref, b_ref, x_hbm, o_hbm, acc_ref, xres_ref, xsem, osem):
    i = pl.program_id(0)
    j = pl.program_id(1)
    row0 = i * FFN_BM

    def x_copy(c):
        return pltpu.make_async_copy(
            x_hbm.at[pl.ds(row0 + c * LN_ROWS, LN_ROWS), :], xres_ref.at[c % 2], xsem.at[c % 2])

    def o_copy(c):
        return pltpu.make_async_copy(
            acc_ref.at[pl.ds(c * LN_ROWS, LN_ROWS), :],
            o_hbm.at[pl.ds(row0 + c * LN_ROWS, LN_ROWS), :], osem.at[c])

    @pl.when(j == 0)
    def _():
        def rows(r, carry):
            r0 = pl.multiple_of(r * FFN_ZERO_ROWS, FFN_ZERO_ROWS)
            acc_ref[pl.ds(r0, FFN_ZERO_ROWS), :] = jnp.zeros((FFN_ZERO_ROWS, D_MODEL), F32)
            return carry

        lax.fori_loop(0, FFN_BM // FFN_ZERO_ROWS, rows, 0)

    @pl.when(j == FFN_NJ - 1)
    def _():
        x_copy(0).start()
        x_copy(1).start()

    xb = xb_ref[...]
    gate = _dot(xb, wg_ref[...])
    up = _dot(xb, wu_ref[...])
    a = (gate * jax.nn.sigmoid(gate) * up).astype(BF16)
    acc_ref[...] += _dot(a, wd_ref[...])

    @pl.when(j == FFN_NJ - 1)
    def _():
        for c in range(FFN_EP):
            rs = slice(c * LN_ROWS, (c + 1) * LN_ROWS)
            x_copy(c).wait()
            z = ALPHA * xres_ref[c % 2] + acc_ref[rs, :]
            acc_ref[rs, :] = _ln_rows(z, g_ref[...], b_ref[...])
            if c + 2 < FFN_EP:
                x_copy(c + 2).start()
            o_copy(c).start()
        for c in range(FFN_EP):
            o_copy(c).wait()


def _ffn_call(x2, x2b, w_gate, w_up, w_down, ln_g, ln_b):
    t = x2.shape[0]
    return pl.pallas_call(
        _ffn_body,
        grid=(t // FFN_BM, FFN_NJ),
        in_specs=[
            pl.BlockSpec((FFN_BM, D_MODEL), lambda i, j: (i, 0)),
            pl.BlockSpec((D_MODEL, FFN_TF), lambda i, j: (0, j)),
            pl.BlockSpec((D_MODEL, FFN_TF), lambda i, j: (0, j)),
            pl.BlockSpec((FFN_TF, D_MODEL), lambda i, j: (j, 0)),
            pl.BlockSpec((1, D_MODEL), lambda i, j: (0, 0)),
            pl.BlockSpec((1, D_MODEL), lambda i, j: (0, 0)),
            pl.BlockSpec(memory_space=pl.ANY),
        ],
        out_specs=pl.BlockSpec(memory_space=pl.ANY),
        out_shape=jax.ShapeDtypeStruct((t, D_MODEL), F32),
        scratch_shapes=[
            pltpu.VMEM((FFN_BM, D_MODEL), F32),
            pltpu.VMEM((2, LN_ROWS, D_MODEL), F32),
            pltpu.SemaphoreType.DMA((2,)),
            pltpu.SemaphoreType.DMA((FFN_EP,)),
        ],
        compiler_params=_params(("arbitrary", "arbitrary")),
        name="ffn",
    )(x2b, w_gate, w_up, w_down, ln_g.reshape(1, D_MODEL), ln_b.reshape(1, D_MODEL), x2)


def _layer_tail(x, qkv, u, kx, vx, bias, w, seq_len):
    attn_out = _attn_call(qkv, bias, w["sink"], seq_len)
    conv_out = _conv_call(u, w["conv_w"], w["conv_b"], w["conv_ln_g"], w["conv_ln_b"], seq_len)
    z1 = _mixout_call(conv_out, attn_out, w["w_out"], x)
    x2, x2b = _xattn_call(z1, w["ln1_g"], w["ln1_b"], w["xq_w"], kx, vx, w["xo_w"], w["ln2_g"],
                          w["ln2_b"], seq_len)
    return _ffn_call(x2, x2b, w["w_gate"], w["w_up"], w["w_down"], w["ln3_g"], w["ln3_b"])


def kernel(x_prompt, x_sample, mem_prompt, mem_sample, rel_bias, w_in, conv_w, conv_b, conv_ln_g,
           conv_ln_b, sink, w_out, ln1_g, ln1_b, xq_w, xk_w, xv_w, xo_w, ln2_g, ln2_b, w_gate, w_up,
           w_down, ln3_g, ln3_b):
    bp, sp, _ = x_prompt.shape
    bs, ss, _ = x_sample.shape
    xs = [x_prompt.reshape(bp * sp, D_MODEL), x_sample.reshape(bs * ss, D_MODEL)]
    mems = [mem_prompt, mem_sample]
    bias = _bias_call(rel_bias)
    for l in range(DEPTH):
        w_in_b = w_in[l].astype(BF16)
        qkv_p, xb_p = _qkv_call(xs[0], w_in_b)
        qkv_s, xb_s = _qkv_call(xs[1], w_in_b)
        u_p, w_gate_b = _glu_call(xb_p, w_in_b, w_gate[l])
        u_s, w_up_b = _glu_call(xb_s, w_in_b, w_up[l])
        w = {
            "conv_w": conv_w[l], "conv_b": conv_b[l],
            "conv_ln_g": conv_ln_g[l], "conv_ln_b": conv_ln_b[l], "sink": sink[l],
            "w_out": w_out[l].astype(BF16), "ln1_g": ln1_g[l], "ln1_b": ln1_b[l],
            "xq_w": xq_w[l].astype(BF16), "xo_w": xo_w[l].astype(BF16),
            "ln2_g": ln2_g[l], "ln2_b": ln2_b[l],
            "w_gate": w_gate_b, "w_up": w_up_b, "w_down": w_down[l].astype(BF16),
            "ln3_g": ln3_g[l], "ln3_b": ln3_b[l],
        }
        wkv = jnp.concatenate([xk_w[l], xv_w[l]], axis=-1).astype(BF16)
        memb = jnp.concatenate([m.reshape(-1, D_MODEL) for m in mems], axis=0).astype(BF16)
        kv = _mm_call(memb, wkv, 0, 2 * X_WIDTH, MEM_LEN, X_WIDTH, "memkv")
        kv = kv.reshape(bp + bs, MEM_LEN, 2 * X_WIDTH)
        kx, vx = kv[..., :X_WIDTH], kv[..., X_WIDTH:]
        xs = [
            _layer_tail(xs[0], qkv_p, u_p, kx[:bp], vx[:bp], bias, w, sp),
            _layer_tail(xs[1], qkv_s, u_s, kx[bp:], vx[bp:], bias, w, ss),
        ]
    return (xs[0].reshape(bp, sp, D_MODEL), xs[1].reshape(bs, ss, D_MODEL))
```

```python
import functools
import math

import jax
import jax.numpy as jnp
from jax import lax
from jax.experimental import pallas as pl
from jax.experimental.pallas import tpu as pltpu

D_MODEL = 4096
HEAD_DIM = 128
CONV_WIDTH = 2048
ATTN_WIDTH = 2048
N_HEADS = 16
N_KV_HEADS = 4
GROUP = 4
KV_WIDTH = 512
CONV_K = 31
WINDOW = 128
BLOCK = 128
N_BUCKETS = 32
MAX_DISTANCE = 128
MEM_LEN = 256
X_HEADS = 4
X_WIDTH = 512
D_FF = 11008
DEPTH = 1
ALPHA = (2 * DEPTH) ** 0.25
LN_EPS = 1e-5
NEG_INF = -1e30
SCALE = HEAD_DIM ** -0.5
LOG2E = math.log2(math.e)

V7X_VMEM_BYTES = 64 * 1024 * 1024
VMEM_LIMIT = V7X_VMEM_BYTES - 8 * 1024 * 1024
LANES = 128
SUBLANES = 8
CONV_HALO = 16

BF16 = jnp.bfloat16
F32 = jnp.float32


def _params(sem):
    return pltpu.CompilerParams(dimension_semantics=sem, vmem_limit_bytes=VMEM_LIMIT)


def _dot(a, b):
    return jnp.dot(a, b, preferred_element_type=F32)


def _dot_nt(a, b):
    return lax.dot_general(a, b, (((1,), (1,)), ((), ())), preferred_element_type=F32)


LN_ROWS = 128


def _ln_rows(z, g, b):
    mu = jnp.mean(z, axis=-1, keepdims=True)
    zc = z - mu
    var = jnp.mean(zc * zc, axis=-1, keepdims=True)
    return zc * lax.rsqrt(var + LN_EPS) * g + b


def _ln_loop(nrows, load, store, g_ref, b_ref):
    def rows(r, carry):
        r0 = pl.multiple_of(r * LN_ROWS, LN_ROWS)
        store(r0, _ln_rows(load(r0), g_ref[...], b_ref[...]))
        return carry

    lax.fori_loop(0, nrows // LN_ROWS, rows, 0)


PROJ_BM = 1024
PROJ_BN = 512
PROJ_ROWS = 256


CAST_SLAB = 256
CAST_SLAB_ROWS = 128


def _cast_job(w, grid):
    nblk = w.shape[1] // CAST_SLAB
    assert w.shape[1] % CAST_SLAB == 0 and math.prod(grid) >= nblk
    spec = pl.BlockSpec((w.shape[0], CAST_SLAB), lambda i, j: (0, jnp.minimum(i * grid[1] + j, nblk - 1)))
    return spec, jax.ShapeDtypeStruct(w.shape, BF16)


def _cast_slab(nblk, w_ref, o_ref):
    step = pl.program_id(0) * pl.num_programs(1) + pl.program_id(1)

    @pl.when(step < nblk)
    def _():
        def rows(r, carry):
            r0 = pl.multiple_of(r * CAST_SLAB_ROWS, CAST_SLAB_ROWS)
            o_ref[pl.ds(r0, CAST_SLAB_ROWS), :] = w_ref[pl.ds(r0, CAST_SLAB_ROWS), :].astype(BF16)
            return carry

        lax.fori_loop(0, w_ref.shape[0] // CAST_SLAB_ROWS, rows, 0)


def _glu_body(nblk, x_ref, wv_ref, wg_ref, wf_ref, o_ref, wb_ref):
    _cast_slab(nblk, wf_ref, wb_ref)
    for r0 in range(0, x_ref.shape[0], PROJ_ROWS):
        x = x_ref[r0:r0 + PROJ_ROWS, :]
        v = _dot(x, wv_ref[...])
        g = _dot(x, wg_ref[...])
        o_ref[r0:r0 + PROJ_ROWS, :] = v * jax.nn.sigmoid(g)


def _glu_call(xb, w_in, w_cast):
    t = xb.shape[0]
    nb = CONV_WIDTH // PROJ_BN
    grid = (t // PROJ_BM, nb)
    cast_spec, cast_shape = _cast_job(w_cast, grid)
    return pl.pallas_call(
        functools.partial(_glu_body, w_cast.shape[1] // CAST_SLAB),
        grid=grid,
        in_specs=[
            pl.BlockSpec((PROJ_BM, D_MODEL), lambda i, j: (i, 0)),
            pl.BlockSpec((D_MODEL, PROJ_BN), lambda i, j: (0, j)),
            pl.BlockSpec((D_MODEL, PROJ_BN), lambda i, j: (0, j + nb)),
            cast_spec,
        ],
        out_specs=[pl.BlockSpec((PROJ_BM, PROJ_BN), lambda i, j: (i, j)), cast_spec],
        out_shape=[jax.ShapeDtypeStruct((t, CONV_WIDTH), F32), cast_shape],
        compiler_params=_params(("arbitrary", "arbitrary")),
        name="glu",
    )(xb, w_in, w_in, w_cast)


QKV_BM = 512
QKV_N = ATTN_WIDTH + 2 * KV_WIDTH
CAST_ROWS = 32


def _qkv_body(x_ref, w_ref, o_ref, xb_ref):
    def rows(r, carry):
        r0 = pl.multiple_of(r * CAST_ROWS, CAST_ROWS)
        xb_ref[pl.ds(r0, CAST_ROWS), :] = x_ref[pl.ds(r0, CAST_ROWS), :].astype(BF16)
        return carry

    lax.fori_loop(0, QKV_BM // CAST_ROWS, rows, 0)
    for c0 in range(0, QKV_N, PROJ_BN):
        o_ref[:, c0:c0 + PROJ_BN] = _dot(xb_ref[...], w_ref[:, c0:c0 + PROJ_BN]).astype(BF16)


def _qkv_call(x, w_qkv):
    t = x.shape[0]
    return pl.pallas_call(
        _qkv_body,
        grid=(t // QKV_BM,),
        in_specs=[
            pl.BlockSpec((QKV_BM, D_MODEL), lambda i: (i, 0)),
            pl.BlockSpec((D_MODEL, QKV_N), lambda i: (0, 0), pipeline_mode=pl.Buffered(1)),
        ],
        out_specs=[
            pl.BlockSpec((QKV_BM, QKV_N), lambda i: (i, 0)),
            pl.BlockSpec((QKV_BM, D_MODEL), lambda i: (i, 0)),
        ],
        out_shape=[
            jax.ShapeDtypeStruct((t, QKV_N), BF16),
            jax.ShapeDtypeStruct((t, D_MODEL), BF16),
        ],
        compiler_params=_params(("parallel",)),
        name="qkv",
    )(x, w_qkv)


def _mm_body(x_ref, w_ref, o_ref):
    o_ref[...] = _dot(x_ref[...], w_ref[...]).astype(o_ref.dtype)


def _mm_call(xb, w, col_off, n, bm, bn, name):
    t, k = xb.shape
    off = col_off // bn
    return pl.pallas_call(
        _mm_body,
        grid=(t // bm, n // bn),
        in_specs=[
            pl.BlockSpec((bm, k), lambda i, j: (i, 0)),
            pl.BlockSpec((k, bn), lambda i, j: (0, j + off)),
        ],
        out_specs=pl.BlockSpec((bm, bn), lambda i, j: (i, j)),
        out_shape=jax.ShapeDtypeStruct((t, n), BF16),
        compiler_params=_params(("parallel", "arbitrary")),
        name=name,
    )(xb, w)


def _bias_body(bucket_ref, rb_ref, o_ref):
    h = pl.program_id(0)
    bkt = bucket_ref[...]
    acc = jnp.zeros(bkt.shape, F32)
    for b in range(N_BUCKETS):
        acc = jnp.where(bkt == b, rb_ref[b, h], acc)
    o_ref[0] = acc * LOG2E


def _t5_bucket(rel):
    nb = N_BUCKETS // 2
    max_exact = nb // 2
    ret = jnp.where(rel > 0, nb, 0)
    n = jnp.abs(rel)
    nf = jnp.maximum(n, 1).astype(F32)
    large = max_exact + (jnp.log(nf / max_exact) / math.log(MAX_DISTANCE / max_exact)
                         * (nb - max_exact)).astype(jnp.int32)
    large = jnp.minimum(large, nb - 1)
    return ret + jnp.where(n < max_exact, n, large)


def _bias_call(rel_bias):
    rel = jnp.arange(3 * BLOCK)[:, None] - BLOCK - jnp.arange(BLOCK)[None, :]
    bucket = _t5_bucket(rel).astype(jnp.int32)
    return pl.pallas_call(
        _bias_body,
        grid=(N_HEADS,),
        in_specs=[
            pl.BlockSpec((3 * BLOCK, BLOCK), lambda h: (0, 0)),
            pl.BlockSpec(memory_space=pltpu.SMEM),
        ],
        out_specs=pl.BlockSpec((1, 3 * BLOCK, BLOCK), lambda h: (h, 0, 0)),
        out_shape=jax.ShapeDtypeStruct((N_HEADS, 3 * BLOCK, BLOCK), F32),
        compiler_params=_params(("arbitrary",)),
        name="t5bias",
    )(bucket, rel_bias)


ATTN_TQ = 512
ATTN_QB = ATTN_TQ // BLOCK


def _attn_body(seq_len, q_ref, kp_ref, km_ref, kn_ref, vp_ref, vm_ref, vn_ref, bias_ref, sink_ref,
               o_ref):
    i = pl.program_id(0)
    t0 = i * ATTN_TQ
    key = lax.broadcasted_iota(jnp.int32, (3 * BLOCK, GROUP * BLOCK), 0)
    qry = lax.broadcasted_iota(jnp.int32, (3 * BLOCK, GROUP * BLOCK), 1) & (BLOCK - 1)
    band = (key >= qry) & (key <= qry + 2 * WINDOW)
    for kh in range(N_KV_HEADS):
        cs = slice(kh * HEAD_DIM, (kh + 1) * HEAD_DIM)
        kall = jnp.concatenate([kp_ref[:, cs], km_ref[:, cs], kn_ref[:, cs]], axis=0)
        vt = jnp.concatenate([vp_ref[:, cs], vm_ref[:, cs], vn_ref[:, cs]], axis=0).T
        bias = jnp.concatenate([bias_ref[kh * GROUP + g] for g in range(GROUP)], axis=1)
        sk = jnp.concatenate(
            [jnp.full((1, BLOCK), sink_ref[kh * GROUP + g] * LOG2E, F32) for g in range(GROUP)], axis=1)
        for qb in range(ATTN_QB):
            rs = slice(qb * BLOCK, (qb + 1) * BLOCK)
            pos = t0 + qb * BLOCK
            first = lax.rem(pos, seq_len) == 0
            last = lax.rem(pos + BLOCK, seq_len) == 0
            lo = jnp.where(first, BLOCK, 0)
            hi = jnp.where(last, 2 * BLOCK, 3 * BLOCK)
            valid = band & (key >= lo) & (key < hi)
            kb = kall[qb * BLOCK:(qb + 3) * BLOCK]
            vb = vt[:, qb * BLOCK:(qb + 3) * BLOCK]
            for g in range(GROUP):
                h = kh * GROUP + g
                ls = slice(g * BLOCK, (g + 1) * BLOCK)
                q = q_ref[rs, h * HEAD_DIM:(h + 1) * HEAD_DIM]
                s = _dot_nt(kb, q) * (SCALE * LOG2E)
                s = jnp.where(valid[:, ls], s + bias[:, ls], NEG_INF)
                m = jnp.maximum(jnp.max(s, axis=0, keepdims=True), sk[:, ls])
                p = jnp.exp2(s - m)
                denom = jnp.sum(p, axis=0, keepdims=True) + jnp.exp2(sk[:, ls] - m)
                p = (p * (1.0 / denom)).astype(BF16)
                ot = _dot(vb, p)
                o_ref[rs, h * HEAD_DIM:(h + 1) * HEAD_DIM] = ot.T.astype(BF16)


def _attn_call(qkv, bias, sink, seq_len):
    t = qkv.shape[0]
    nblk = t // BLOCK
    kcol = ATTN_WIDTH // KV_WIDTH
    vcol = kcol + 1

    def prev_map(c):
        return lambda i: (jnp.maximum(i * ATTN_QB - 1, 0), c)

    def next_map(c):
        return lambda i: (jnp.minimum((i + 1) * ATTN_QB, nblk - 1), c)

    return pl.pallas_call(
        functools.partial(_attn_body, seq_len),
        grid=(t // ATTN_TQ,),
        in_specs=[
            pl.BlockSpec((ATTN_TQ, ATTN_WIDTH), lambda i: (i, 0)),
            pl.BlockSpec((BLOCK, KV_WIDTH), prev_map(kcol)),
            pl.BlockSpec((ATTN_TQ, KV_WIDTH), lambda i: (i, kcol)),
            pl.BlockSpec((BLOCK, KV_WIDTH), next_map(kcol)),
            pl.BlockSpec((BLOCK, KV_WIDTH), prev_map(vcol)),
            pl.BlockSpec((ATTN_TQ, KV_WIDTH), lambda i: (i, vcol)),
            pl.BlockSpec((BLOCK, KV_WIDTH), next_map(vcol)),
            pl.BlockSpec((N_HEADS, 3 * BLOCK, BLOCK), lambda i: (0, 0, 0)),
            pl.BlockSpec(memory_space=pltpu.SMEM),
        ],
        out_specs=pl.BlockSpec((ATTN_TQ, ATTN_WIDTH), lambda i: (i, 0)),
        out_shape=jax.ShapeDtypeStruct((t, ATTN_WIDTH), BF16),
        compiler_params=_params(("parallel",)),
        name="attn",
    )(qkv, qkv, qkv, qkv, qkv, qkv, qkv, bias, sink)


CONV_TS = 512
CONV_RC = 64
CONV_NC = CONV_WIDTH // LANES


def _conv_body(seq_len, up_ref, um_ref, un_ref, w_ref, cb_ref, g_ref, b_ref, o_ref, buf_ref, h_ref):
    i = pl.program_id(0)
    t0 = i * CONV_TS
    first = lax.rem(t0, seq_len) == 0
    last = lax.rem(t0 + CONV_TS, seq_len) == 0
    shift = CONV_HALO - CONV_K // 2
    groups = CONV_RC // SUBLANES

    for c in range(CONV_NC):
        cs = slice(c * LANES, (c + 1) * LANES)
        buf_ref[c, 0:CONV_HALO, :] = jnp.where(first, 0.0, up_ref[:, cs])
        buf_ref[c, CONV_HALO:CONV_HALO + CONV_TS, :] = um_ref[:, cs]
        buf_ref[c, CONV_HALO + CONV_TS:, :] = jnp.where(last, 0.0, un_ref[:, cs])

        def rows(r, carry, c=c, cs=cs):
            r0 = pl.multiple_of(r * CONV_RC, CONV_RC)
            win = [buf_ref[c, pl.ds(r0 + shift + off, SUBLANES), :]
                   for off in range(CONV_K + CONV_RC - SUBLANES)]
            accs = [cb_ref[:, cs]] * groups
            for k in range(CONV_K):
                wk = w_ref[k * SUBLANES:(k + 1) * SUBLANES, cs]
                accs = [accs[j] + win[k + SUBLANES * j] * wk for j in range(groups)]
            for j in range(groups):
                h_ref[pl.ds(r0 + SUBLANES * j, SUBLANES), cs] = accs[j]
            return carry

        lax.fori_loop(0, CONV_TS // CONV_RC, rows, 0)

    def store(r0, y):
        o_ref[pl.ds(r0, LN_ROWS), :] = (y * jax.nn.sigmoid(y)).astype(BF16)

    _ln_loop(CONV_TS, lambda r0: h_ref[pl.ds(r0, LN_ROWS), :], store, g_ref, b_ref)


def _conv_call(u, conv_w, conv_b, ln_g, ln_b, seq_len):
    t = u.shape[0]
    per = CONV_TS // CONV_HALO
    nh = t // CONV_HALO
    w_rep = jnp.repeat(conv_w, SUBLANES, axis=0)
    cb_rep = jnp.broadcast_to(conv_b[None, :], (SUBLANES, CONV_WIDTH))
    full = lambda a: pl.BlockSpec(a.shape, lambda i: (0, 0))
    g2, b2 = ln_g.reshape(1, CONV_WIDTH), ln_b.reshape(1, CONV_WIDTH)
    return pl.pallas_call(
        functools.partial(_conv_body, seq_len),
        grid=(t // CONV_TS,),
        in_specs=[
            pl.BlockSpec((CONV_HALO, CONV_WIDTH), lambda i: (jnp.maximum(i * per - 1, 0), 0)),
            pl.BlockSpec((CONV_TS, CONV_WIDTH), lambda i: (i, 0)),
            pl.BlockSpec((CONV_HALO, CONV_WIDTH), lambda i: (jnp.minimum((i + 1) * per, nh - 1), 0)),
            full(w_rep), full(cb_rep), full(g2), full(b2),
        ],
        out_specs=pl.BlockSpec((CONV_TS, CONV_WIDTH), lambda i: (i, 0)),
        out_shape=jax.ShapeDtypeStruct((t, CONV_WIDTH), BF16),
        scratch_shapes=[
            pltpu.VMEM((CONV_NC, CONV_TS + 2 * CONV_HALO, LANES), F32),
            pltpu.VMEM((CONV_TS, CONV_WIDTH), F32),
        ],
        compiler_params=_params(("parallel",)),
        name="conv",
    )(u, u, u, w_rep, cb_rep, g2, b2)


def _mixout_body(c_ref, a_ref, wc_ref, wa_ref, x_ref, o_ref):
    mix = _dot(c_ref[...], wc_ref[...]) + _dot(a_ref[...], wa_ref[...])
    o_ref[...] = ALPHA * x_ref[...] + mix


MIX_BN = 1024


def _mixout_call(conv_out, attn_out, w_out, x):
    t = x.shape[0]
    return pl.pallas_call(
        _mixout_body,
        grid=(t // PROJ_BM, D_MODEL // MIX_BN),
        in_specs=[
            pl.BlockSpec((PROJ_BM, CONV_WIDTH), lambda i, j: (i, 0)),
            pl.BlockSpec((PROJ_BM, ATTN_WIDTH), lambda i, j: (i, 0)),
            pl.BlockSpec((CONV_WIDTH, MIX_BN), lambda i, j: (0, j)),
            pl.BlockSpec((ATTN_WIDTH, MIX_BN), lambda i, j: (1, j)),
            pl.BlockSpec((PROJ_BM, MIX_BN), lambda i, j: (i, j)),
        ],
        out_specs=pl.BlockSpec((PROJ_BM, MIX_BN), lambda i, j: (i, j)),
        out_shape=jax.ShapeDtypeStruct((t, D_MODEL), F32),
        compiler_params=_params(("parallel", "arbitrary")),
        name="mixout",
    )(conv_out, attn_out, w_out, w_out, x)


XA_BM = 256


def _xattn_body(z_ref, g1_ref, b1_ref, wq_ref, k_ref, v_ref, wo_ref, g2_ref, b2_ref, o_ref, ob_ref,
                x1_ref, x1b_ref, z2_ref):
    def store_x1(r0, x1):
        x1_ref[pl.ds(r0, LN_ROWS), :] = x1
        x1b_ref[pl.ds(r0, LN_ROWS), :] = x1.astype(BF16)

    _ln_loop(XA_BM, lambda r0: z_ref[pl.ds(r0, LN_ROWS), :], store_x1, g1_ref, b1_ref)

    q = _dot(x1b_ref[...], wq_ref[...])
    outs = []
    for h in range(X_HEADS):
        cs = slice(h * HEAD_DIM, (h + 1) * HEAD_DIM)
        s = _dot_nt(q[:, cs].astype(BF16), k_ref[0, :, cs]) * SCALE
        m = jnp.max(s, axis=-1, keepdims=True)
        e = jnp.exp(s - m)
        p = (e / jnp.sum(e, axis=-1, keepdims=True)).astype(BF16)
        outs.append(_dot(p, v_ref[0, :, cs]))
    o = jnp.concatenate(outs, axis=-1).astype(BF16)
    z2_ref[...] = ALPHA * x1_ref[...] + _dot(o, wo_ref[...])

    def store_x2(r0, x2):
        o_ref[pl.ds(r0, LN_ROWS), :] = x2
        ob_ref[pl.ds(r0, LN_ROWS), :] = x2.astype(BF16)

    _ln_loop(XA_BM, lambda r0: z2_ref[pl.ds(r0, LN_ROWS), :], store_x2, g2_ref, b2_ref)


def _xattn_call(z1, ln1_g, ln1_b, xq_w, kx, vx, xo_w, ln2_g, ln2_b, seq_len):
    t = z1.shape[0]
    per = seq_len // XA_BM
    vec = pl.BlockSpec((1, D_MODEL), lambda i: (0, 0))
    return pl.pallas_call(
        _xattn_body,
        grid=(t // XA_BM,),
        in_specs=[
            pl.BlockSpec((XA_BM, D_MODEL), lambda i: (i, 0)),
            vec, vec,
            pl.BlockSpec((D_MODEL, X_WIDTH), lambda i: (0, 0)),
            pl.BlockSpec((1, MEM_LEN, X_WIDTH), lambda i: (i // per, 0, 0)),
            pl.BlockSpec((1, MEM_LEN, X_WIDTH), lambda i: (i // per, 0, 0)),
            pl.BlockSpec((X_WIDTH, D_MODEL), lambda i: (0, 0)),
            vec, vec,
        ],
        out_specs=[
            pl.BlockSpec((XA_BM, D_MODEL), lambda i: (i, 0)),
            pl.BlockSpec((XA_BM, D_MODEL), lambda i: (i, 0)),
        ],
        out_shape=[
            jax.ShapeDtypeStruct((t, D_MODEL), F32),
            jax.ShapeDtypeStruct((t, D_MODEL), BF16),
        ],
        scratch_shapes=[
            pltpu.VMEM((XA_BM, D_MODEL), F32),
            pltpu.VMEM((XA_BM, D_MODEL), BF16),
            pltpu.VMEM((XA_BM, D_MODEL), F32),
        ],
        compiler_params=_params(("parallel",)),
        name="xattn",
    )(z1, ln1_g.reshape(1, D_MODEL), ln1_b.reshape(1, D_MODEL), xq_w, kx, vx, xo_w,
      ln2_g.reshape(1, D_MODEL), ln2_b.reshape(1, D_MODEL))


FFN_BM = 1024
FFN_TF = 256
FFN_NJ = D_FF // FFN_TF
FFN_EP = FFN_BM // LN_ROWS
FFN_ZERO_ROWS = 64


def _ffn_body(xb_ref, wg_ref, wu_ref, wd_ref, g_ref, b_ref, x_hbm, o_hbm, acc_ref, xres_ref, xsem, osem):
    i = pl.program_id(0)
    j = pl.program_id(1)
    row0 = i * FFN_BM

    def x_copy(c):
        return pltpu.make_async_copy(
            x_hbm.at[pl.ds(row0 + c * LN_ROWS, LN_ROWS), :], xres_ref.at[c % 2], xsem.at[c % 2])

    def o_copy(c):
        return pltpu.make_async_copy(
            acc_ref.at[pl.ds(c * LN_ROWS, LN_ROWS), :],
            o_hbm.at[pl.ds(row0 + c * LN_ROWS, LN_ROWS), :], osem.at[c])

    @pl.when(j == 0)
    def _():
        def rows(r, carry):
            r0 = pl.multiple_of(r * FFN_ZERO_ROWS, FFN_ZERO_ROWS)
            acc_ref[pl.ds(r0, FFN_ZERO_ROWS), :] = jnp.zeros((FFN_ZERO_ROWS, D_MODEL), F32)
            return carry

        lax.fori_loop(0, FFN_BM // FFN_ZERO_ROWS, rows, 0)

    @pl.when(j == FFN_NJ - 1)
    def _():
        x_copy(0).start()
        x_copy(1).start()

    xb = xb_ref[...]
    gate = _dot(xb, wg_ref[...])
    up = _dot(xb, wu_ref[...])
    a = (gate * jax.nn.sigmoid(gate) * up).astype(BF16)
    acc_ref[...] += _dot(a, wd_ref[...])

    @pl.when(j == FFN_NJ - 1)
    def _():
        for c in range(FFN_EP):
            rs = slice(c * LN_ROWS, (c + 1) * LN_ROWS)
            x_copy(c).wait()
            z = ALPHA * xres_ref[c % 2] + acc_ref[rs, :]
            acc_ref[rs, :] = _ln_rows(z, g_ref[...], b_ref[...])
            if c + 2 < FFN_EP:
                x_copy(c + 2).start()
            o_copy(c).start()
        for c in range(FFN_EP):
            o_copy(c).wait()


def _ffn_call(x2, x2b, w_gate, w_up, w_down, ln_g, ln_b):
    t = x2.shape[0]
    return pl.pallas_call(
        _ffn_body,
        grid=(t // FFN_BM, FFN_NJ),
        in_specs=[
            pl.BlockSpec((FFN_BM, D_MODEL), lambda i, j: (i, 0)),
            pl.BlockSpec((D_MODEL, FFN_TF), lambda i, j: (0, j)),
            pl.BlockSpec((D_MODEL, FFN_TF), lambda i, j: (0, j)),
            pl.BlockSpec((FFN_TF, D_MODEL), lambda i, j: (j, 0)),
            pl.BlockSpec((1, D_MODEL), lambda i, j: (0, 0)),
            pl.BlockSpec((1, D_MODEL), lambda i, j: (0, 0)),
            pl.BlockSpec(memory_space=pl.ANY),
        ],
        out_specs=pl.BlockSpec(memory_space=pl.ANY),
        out_shape=jax.ShapeDtypeStruct((t, D_MODEL), F32),
        scratch_shapes=[
            pltpu.VMEM((FFN_BM, D_MODEL), F32),
            pltpu.VMEM((2, LN_ROWS, D_MODEL), F32),
            pltpu.SemaphoreType.DMA((2,)),
            pltpu.SemaphoreType.DMA((FFN_EP,)),
        ],
        compiler_params=_params(("arbitrary", "arbitrary")),
        name="ffn",
    )(x2b, w_gate, w_up, w_down, ln_g.reshape(1, D_MODEL), ln_b.reshape(1, D_MODEL), x2)


def _layer_tail(x, qkv, u, kx, vx, bias, w, seq_len):
    attn_out = _attn_call(qkv, bias, w["sink"], seq_len)
    conv_out = _conv_call(u, w["conv_w"], w["conv_b"], w["conv_ln_g"], w["conv_ln_b"], seq_len)
    z1 = _mixout_call(conv_out, attn_out, w["w_out"], x)
    x2, x2b = _xattn_call(z1, w["ln1_g"], w["ln1_b"], w["xq_w"], kx, vx, w["xo_w"], w["ln2_g"],
                          w["ln2_b"], seq_len)
    return _ffn_call(x2, x2b, w["w_gate"], w["w_up"], w["w_down"], w["ln3_g"], w["ln3_b"])


def kernel(x_prompt, x_sample, mem_prompt, mem_sample, rel_bias, w_in, conv_w, conv_b, conv_ln_g,
           conv_ln_b, sink, w_out, ln1_g, ln1_b, xq_w, xk_w, xv_w, xo_w, ln2_g, ln2_b, w_gate, w_up,
           w_down, ln3_g, ln3_b):
    bp, sp, _ = x_prompt.shape
    bs, ss, _ = x_sample.shape
    xs = [x_prompt.reshape(bp * sp, D_MODEL), x_sample.reshape(bs * ss, D_MODEL)]
    mems = [mem_prompt, mem_sample]
    bias = _bias_call(rel_bias)
    for l in range(DEPTH):
        w_glu_b = w_in[l][:, :2 * CONV_WIDTH].astype(BF16)
        w_qkv_b = w_in[l][:, 2 * CONV_WIDTH:].astype(BF16)
        qkv_p, xb_p = _qkv_call(xs[0], w_qkv_b)
        qkv_s, xb_s = _qkv_call(xs[1], w_qkv_b)
        u_p, w_gate_b = _glu_call(xb_p, w_glu_b, w_gate[l])
        u_s, w_up_b = _glu_call(xb_s, w_glu_b, w_up[l])
        w = {
            "conv_w": conv_w[l], "conv_b": conv_b[l],
            "conv_ln_g": conv_ln_g[l], "conv_ln_b": conv_ln_b[l], "sink": sink[l],
            "w_out": w_out[l].astype(BF16), "ln1_g": ln1_g[l], "ln1_b": ln1_b[l],
            "xq_w": xq_w[l].astype(BF16), "xo_w": xo_w[l].astype(BF16),
            "ln2_g": ln2_g[l], "ln2_b": ln2_b[l],
            "w_gate": w_gate_b, "w_up": w_up_b, "w_down": w_down[l].astype(BF16),
            "ln3_g": ln3_g[l], "ln3_b": ln3_b[l],
        }
        wkv = jnp.concatenate([xk_w[l], xv_w[l]], axis=-1).astype(BF16)
        memb = jnp.concatenate([m.reshape(-1, D_MODEL) for m in mems], axis=0).astype(BF16)
        kv = _mm_call(memb, wkv, 0, 2 * X_WIDTH, MEM_LEN, X_WIDTH, "memkv")
        kv = kv.reshape(bp + bs, MEM_LEN, 2 * X_WIDTH)
        kx, vx = kv[..., :X_WIDTH], kv[..., X_WIDTH:]
        xs = [
            _layer_tail(xs[0], qkv_p, u_p, kx[:bp], vx[:bp], bias, w, sp),
            _layer_tail(xs[1], qkv_s, u_s, kx[bp:], vx[bp:], bias, w, ss),
        ]
    return (xs[0].reshape(bp, sp, D_MODEL), xs[1].reshape(bs, ss, D_MODEL))
```

```python
import functools
import math

import jax
import jax.numpy as jnp
from jax import lax
from jax.experimental import pallas as pl
from jax.experimental.pallas import tpu as pltpu

D_MODEL = 4096
HEAD_DIM = 128
CONV_WIDTH = 2048
ATTN_WIDTH = 2048
N_HEADS = 16
N_KV_HEADS = 4
GROUP = 4
KV_WIDTH = 512
CONV_K = 31
WINDOW = 128
BLOCK = 128
N_BUCKETS = 32
MAX_DISTANCE = 128
MEM_LEN = 256
X_HEADS = 4
X_WIDTH = 512
D_FF = 11008
DEPTH = 1
ALPHA = (2 * DEPTH) ** 0.25
LN_EPS = 1e-5
NEG_INF = -1e30
SCALE = HEAD_DIM ** -0.5
LOG2E = math.log2(math.e)

V7X_VMEM_BYTES = 64 * 1024 * 1024
VMEM_LIMIT = V7X_VMEM_BYTES - 8 * 1024 * 1024
LANES = 128
SUBLANES = 8
CONV_HALO = 16

BF16 = jnp.bfloat16
F32 = jnp.float32


def _params(sem):
    return pltpu.CompilerParams(dimension_semantics=sem, vmem_limit_bytes=VMEM_LIMIT)


def _dot(a, b):
    return jnp.dot(a, b, preferred_element_type=F32)


def _dot_nt(a, b):
    return lax.dot_general(a, b, (((1,), (1,)), ((), ())), preferred_element_type=F32)


LN_ROWS = 128


def _ln_rows(z, g, b):
    mu = jnp.mean(z, axis=-1, keepdims=True)
    zc = z - mu
    var = jnp.mean(zc * zc, axis=-1, keepdims=True)
    return zc * lax.rsqrt(var + LN_EPS) * g + b


def _ln_loop(nrows, load, store, g_ref, b_ref):
    def rows(r, carry):
        r0 = pl.multiple_of(r * LN_ROWS, LN_ROWS)
        store(r0, _ln_rows(load(r0), g_ref[...], b_ref[...]))
        return carry

    lax.fori_loop(0, nrows // LN_ROWS, rows, 0)


PROJ_BM = 1024
PROJ_BN = 512
PROJ_ROWS = 256


CAST_SLAB = 256
CAST_SLAB_ROWS = 128


def _cast_job(w, grid):
    nblk = w.shape[1] // CAST_SLAB
    assert w.shape[1] % CAST_SLAB == 0 and math.prod(grid) >= nblk
    spec = pl.BlockSpec((w.shape[0], CAST_SLAB), lambda i, j: (0, jnp.minimum(i * grid[1] + j, nblk - 1)))
    return spec, jax.ShapeDtypeStruct(w.shape, BF16)


def _cast_slab(nblk, w_ref, o_ref):
    step = pl.program_id(0) * pl.num_programs(1) + pl.program_id(1)

    @pl.when(step < nblk)
    def _():
        def rows(r, carry):
            r0 = pl.multiple_of(r * CAST_SLAB_ROWS, CAST_SLAB_ROWS)
            o_ref[pl.ds(r0, CAST_SLAB_ROWS), :] = w_ref[pl.ds(r0, CAST_SLAB_ROWS), :].astype(BF16)
            return carry

        lax.fori_loop(0, w_ref.shape[0] // CAST_SLAB_ROWS, rows, 0)


def _glu_body(nblk, x_ref, wv_ref, wg_ref, wf_ref, o_ref, wb_ref):
    _cast_slab(nblk, wf_ref, wb_ref)
    for r0 in range(0, x_ref.shape[0], PROJ_ROWS):
        x = x_ref[r0:r0 + PROJ_ROWS, :]
        v = _dot(x, wv_ref[...])
        g = _dot(x, wg_ref[...])
        o_ref[r0:r0 + PROJ_ROWS, :] = v * jax.nn.sigmoid(g)


def _glu_call(xb, w_in, w_cast):
    t = xb.shape[0]
    nb = CONV_WIDTH // PROJ_BN
    grid = (t // PROJ_BM, nb)
    cast_spec, cast_shape = _cast_job(w_cast, grid)
    return pl.pallas_call(
        functools.partial(_glu_body, w_cast.shape[1] // CAST_SLAB),
        grid=grid,
        in_specs=[
            pl.BlockSpec((PROJ_BM, D_MODEL), lambda i, j: (i, 0)),
            pl.BlockSpec((D_MODEL, PROJ_BN), lambda i, j: (0, j)),
            pl.BlockSpec((D_MODEL, PROJ_BN), lambda i, j: (0, j + nb)),
            cast_spec,
        ],
        out_specs=[pl.BlockSpec((PROJ_BM, PROJ_BN), lambda i, j: (i, j)), cast_spec],
        out_shape=[jax.ShapeDtypeStruct((t, CONV_WIDTH), F32), cast_shape],
        compiler_params=_params(("arbitrary", "arbitrary")),
        name="glu",
    )(xb, w_in, w_in, w_cast)


QKV_BM = 512
QKV_N = ATTN_WIDTH + 2 * KV_WIDTH
CAST_ROWS = 32


def _qkv_body(x_ref, w_ref, o_ref, xb_ref):
    def rows(r, carry):
        r0 = pl.multiple_of(r * CAST_ROWS, CAST_ROWS)
        xb_ref[pl.ds(r0, CAST_ROWS), :] = x_ref[pl.ds(r0, CAST_ROWS), :].astype(BF16)
        return carry

    lax.fori_loop(0, QKV_BM // CAST_ROWS, rows, 0)
    for c0 in range(0, QKV_N, PROJ_BN):
        o_ref[:, c0:c0 + PROJ_BN] = _dot(xb_ref[...], w_ref[:, c0:c0 + PROJ_BN]).astype(BF16)


def _qkv_call(x, w_qkv):
    t = x.shape[0]
    return pl.pallas_call(
        _qkv_body,
        grid=(t // QKV_BM,),
        in_specs=[
            pl.BlockSpec((QKV_BM, D_MODEL), lambda i: (i, 0)),
            pl.BlockSpec((D_MODEL, QKV_N), lambda i: (0, 0), pipeline_mode=pl.Buffered(1)),
        ],
        out_specs=[
            pl.BlockSpec((QKV_BM, QKV_N), lambda i: (i, 0)),
            pl.BlockSpec((QKV_BM, D_MODEL), lambda i: (i, 0)),
        ],
        out_shape=[
            jax.ShapeDtypeStruct((t, QKV_N), BF16),
            jax.ShapeDtypeStruct((t, D_MODEL), BF16),
        ],
        compiler_params=_params(("parallel",)),
        name="qkv",
    )(x, w_qkv)


def _mm_body(x_ref, w_ref, o_ref):
    o_ref[...] = _dot(x_ref[...], w_ref[...]).astype(o_ref.dtype)


def _mm_call(xb, w, col_off, n, bm, bn, name):
    t, k = xb.shape
    off = col_off // bn
    return pl.pallas_call(
        _mm_body,
        grid=(t // bm, n // bn),
        in_specs=[
            pl.BlockSpec((bm, k), lambda i, j: (i, 0)),
            pl.BlockSpec((k, bn), lambda i, j: (0, j + off)),
        ],
        out_specs=pl.BlockSpec((bm, bn), lambda i, j: (i, j)),
        out_shape=jax.ShapeDtypeStruct((t, n), BF16),
        compiler_params=_params(("parallel", "arbitrary")),
        name=name,
    )(xb, w)


def _bias_body(bucket_ref, rb_ref, o_ref):
    h = pl.program_id(0)
    bkt = bucket_ref[...]
    acc = jnp.zeros(bkt.shape, F32)
    for b in range(N_BUCKETS):
        acc = jnp.where(bkt == b, rb_ref[b, h], acc)
    o_ref[0] = acc * LOG2E


def _t5_bucket(rel):
    nb = N_BUCKETS // 2
    max_exact = nb // 2
    ret = jnp.where(rel > 0, nb, 0)
    n = jnp.abs(rel)
    nf = jnp.maximum(n, 1).astype(F32)
    large = max_exact + (jnp.log(nf / max_exact) / math.log(MAX_DISTANCE / max_exact)
                         * (nb - max_exact)).astype(jnp.int32)
    large = jnp.minimum(large, nb - 1)
    return ret + jnp.where(n < max_exact, n, large)


def _bias_call(rel_bias):
    rel = jnp.arange(3 * BLOCK)[:, None] - BLOCK - jnp.arange(BLOCK)[None, :]
    bucket = _t5_bucket(rel).astype(jnp.int32)
    return pl.pallas_call(
        _bias_body,
        grid=(N_HEADS,),
        in_specs=[
            pl.BlockSpec((3 * BLOCK, BLOCK), lambda h: (0, 0)),
            pl.BlockSpec(memory_space=pltpu.SMEM),
        ],
        out_specs=pl.BlockSpec((1, 3 * BLOCK, BLOCK), lambda h: (h, 0, 0)),
        out_shape=jax.ShapeDtypeStruct((N_HEADS, 3 * BLOCK, BLOCK), F32),
        compiler_params=_params(("arbitrary",)),
        name="t5bias",
    )(bucket, rel_bias)


ATTN_TQ = 512
ATTN_QB = ATTN_TQ // BLOCK


ATTN_CAST_SLABS = 16


def _attn_body(seq_len, q_ref, kp_ref, km_ref, kn_ref, vp_ref, vm_ref, vn_ref, bias_ref, sink_ref,
               *rest):
    i = pl.program_id(0)
    if len(rest) == 3:
        wf_ref, o_ref, wb_ref = rest

        @pl.when(i < ATTN_CAST_SLABS)
        def _():
            def rows(r, carry):
                r0 = pl.multiple_of(r * 16, 16)
                wb_ref[pl.ds(r0, 16), :] = wf_ref[pl.ds(r0, 16), :].astype(BF16)
                return carry

            lax.fori_loop(0, wf_ref.shape[0] // 16, rows, 0)
    else:
        (o_ref,) = rest
    t0 = i * ATTN_TQ
    key = lax.broadcasted_iota(jnp.int32, (3 * BLOCK, GROUP * BLOCK), 0)
    qry = lax.broadcasted_iota(jnp.int32, (3 * BLOCK, GROUP * BLOCK), 1) & (BLOCK - 1)
    band = (key >= qry) & (key <= qry + 2 * WINDOW)
    for kh in range(N_KV_HEADS):
        cs = slice(kh * HEAD_DIM, (kh + 1) * HEAD_DIM)
        kall = jnp.concatenate([kp_ref[:, cs], km_ref[:, cs], kn_ref[:, cs]], axis=0)
        vt = jnp.concatenate([vp_ref[:, cs], vm_ref[:, cs], vn_ref[:, cs]], axis=0).T
        bias = jnp.concatenate([bias_ref[kh * GROUP + g] for g in range(GROUP)], axis=1)
        sk = jnp.concatenate(
            [jnp.full((1, BLOCK), sink_ref[kh * GROUP + g] * LOG2E, F32) for g in range(GROUP)], axis=1)
        for qb in range(ATTN_QB):
            rs = slice(qb * BLOCK, (qb + 1) * BLOCK)
            pos = t0 + qb * BLOCK
            first = lax.rem(pos, seq_len) == 0
            last = lax.rem(pos + BLOCK, seq_len) == 0
            lo = jnp.where(first, BLOCK, 0)
            hi = jnp.where(last, 2 * BLOCK, 3 * BLOCK)
            valid = band & (key >= lo) & (key < hi)
            kb = kall[qb * BLOCK:(qb + 3) * BLOCK]
            vb = vt[:, qb * BLOCK:(qb + 3) * BLOCK]
            for g in range(GROUP):
                h = kh * GROUP + g
                ls = slice(g * BLOCK, (g + 1) * BLOCK)
                q = q_ref[rs, h * HEAD_DIM:(h + 1) * HEAD_DIM]
                s = _dot_nt(kb, q) * (SCALE * LOG2E)
                s = jnp.where(valid[:, ls], s + bias[:, ls], NEG_INF)
                m = jnp.maximum(jnp.max(s, axis=0, keepdims=True), sk[:, ls])
                p = jnp.exp2(s - m)
                denom = jnp.sum(p, axis=0, keepdims=True) + jnp.exp2(sk[:, ls] - m)
                p = (p * (1.0 / denom)).astype(BF16)
                ot = _dot(vb, p)
                o_ref[rs, h * HEAD_DIM:(h + 1) * HEAD_DIM] = ot.T.astype(BF16)


def _attn_call(qkv, bias, sink, seq_len, w_cast=None):
    t = qkv.shape[0]
    nblk = t // BLOCK
    kcol = ATTN_WIDTH // KV_WIDTH
    vcol = kcol + 1

    def prev_map(c):
        return lambda i: (jnp.maximum(i * ATTN_QB - 1, 0), c)

    def next_map(c):
        return lambda i: (jnp.minimum((i + 1) * ATTN_QB, nblk - 1), c)

    in_specs = [
        pl.BlockSpec((ATTN_TQ, ATTN_WIDTH), lambda i: (i, 0)),
        pl.BlockSpec((BLOCK, KV_WIDTH), prev_map(kcol)),
        pl.BlockSpec((ATTN_TQ, KV_WIDTH), lambda i: (i, kcol)),
        pl.BlockSpec((BLOCK, KV_WIDTH), next_map(kcol)),
        pl.BlockSpec((BLOCK, KV_WIDTH), prev_map(vcol)),
        pl.BlockSpec((ATTN_TQ, KV_WIDTH), lambda i: (i, vcol)),
        pl.BlockSpec((BLOCK, KV_WIDTH), next_map(vcol)),
        pl.BlockSpec((N_HEADS, 3 * BLOCK, BLOCK), lambda i: (0, 0, 0)),
        pl.BlockSpec(memory_space=pltpu.SMEM),
    ]
    out_specs = [pl.BlockSpec((ATTN_TQ, ATTN_WIDTH), lambda i: (i, 0))]
    out_shape = [jax.ShapeDtypeStruct((t, ATTN_WIDTH), BF16)]
    args = [qkv] * 7 + [bias, sink]
    if w_cast is not None:
        slab = w_cast.shape[0] // ATTN_CAST_SLABS
        assert slab * ATTN_CAST_SLABS == w_cast.shape[0] and slab % 16 == 0
        assert t // ATTN_TQ >= ATTN_CAST_SLABS
        cast_spec = pl.BlockSpec((slab, w_cast.shape[1]),
                                 lambda i: (jnp.minimum(i, ATTN_CAST_SLABS - 1), 0))
        in_specs.append(cast_spec)
        out_specs.append(cast_spec)
        out_shape.append(jax.ShapeDtypeStruct(w_cast.shape, BF16))
        args.append(w_cast)
    return pl.pallas_call(
        functools.partial(_attn_body, seq_len),
        grid=(t // ATTN_TQ,),
        in_specs=in_specs,
        out_specs=out_specs,
        out_shape=out_shape,
        compiler_params=_params(("arbitrary",)),
        name="attn",
    )(*args)


CONV_TS = 512
CONV_RC = 64
CONV_NC = CONV_WIDTH // LANES


def _conv_body(seq_len, up_ref, um_ref, un_ref, w_ref, cb_ref, g_ref, b_ref, o_ref, buf_ref, h_ref):
    i = pl.program_id(0)
    t0 = i * CONV_TS
    first = lax.rem(t0, seq_len) == 0
    last = lax.rem(t0 + CONV_TS, seq_len) == 0
    shift = CONV_HALO - CONV_K // 2
    groups = CONV_RC // SUBLANES

    for c in range(CONV_NC):
        cs = slice(c * LANES, (c + 1) * LANES)
        buf_ref[c, 0:CONV_HALO, :] = jnp.where(first, 0.0, up_ref[:, cs])
        buf_ref[c, CONV_HALO:CONV_HALO + CONV_TS, :] = um_ref[:, cs]
        buf_ref[c, CONV_HALO + CONV_TS:, :] = jnp.where(last, 0.0, un_ref[:, cs])

        def rows(r, carry, c=c, cs=cs):
            r0 = pl.multiple_of(r * CONV_RC, CONV_RC)
            win = [buf_ref[c, pl.ds(r0 + shift + off, SUBLANES), :]
                   for off in range(CONV_K + CONV_RC - SUBLANES)]
            accs = [cb_ref[:, cs]] * groups
            for k in range(CONV_K):
                wk = w_ref[k * SUBLANES:(k + 1) * SUBLANES, cs]
                accs = [accs[j] + win[k + SUBLANES * j] * wk for j in range(groups)]
            for j in range(groups):
                h_ref[pl.ds(r0 + SUBLANES * j, SUBLANES), cs] = accs[j]
            return carry

        lax.fori_loop(0, CONV_TS // CONV_RC, rows, 0)

    def store(r0, y):
        o_ref[pl.ds(r0, LN_ROWS), :] = (y * jax.nn.sigmoid(y)).astype(BF16)

    _ln_loop(CONV_TS, lambda r0: h_ref[pl.ds(r0, LN_ROWS), :], store, g_ref, b_ref)


def _conv_call(u, conv_w, conv_b, ln_g, ln_b, seq_len):
    t = u.shape[0]
    per = CONV_TS // CONV_HALO
    nh = t // CONV_HALO
    w_rep = jnp.repeat(conv_w, SUBLANES, axis=0)
    cb_rep = jnp.broadcast_to(conv_b[None, :], (SUBLANES, CONV_WIDTH))
    full = lambda a: pl.BlockSpec(a.shape, lambda i: (0, 0))
    g2, b2 = ln_g.reshape(1, CONV_WIDTH), ln_b.reshape(1, CONV_WIDTH)
    return pl.pallas_call(
        functools.partial(_conv_body, seq_len),
        grid=(t // CONV_TS,),
        in_specs=[
            pl.BlockSpec((CONV_HALO, CONV_WIDTH), lambda i: (jnp.maximum(i * per - 1, 0), 0)),
            pl.BlockSpec((CONV_TS, CONV_WIDTH), lambda i: (i, 0)),
            pl.BlockSpec((CONV_HALO, CONV_WIDTH), lambda i: (jnp.minimum((i + 1) * per, nh - 1), 0)),
            full(w_rep), full(cb_rep), full(g2), full(b2),
        ],
        out_specs=pl.BlockSpec((CONV_TS, CONV_WIDTH), lambda i: (i, 0)),
        out_shape=jax.ShapeDtypeStruct((t, CONV_WIDTH), BF16),
        scratch_shapes=[
            pltpu.VMEM((CONV_NC, CONV_TS + 2 * CONV_HALO, LANES), F32),
            pltpu.VMEM((CONV_TS, CONV_WIDTH), F32),
        ],
        compiler_params=_params(("parallel",)),
        name="conv",
    )(u, u, u, w_rep, cb_rep, g2, b2)


def _mixout_body(c_ref, a_ref, wc_ref, wa_ref, x_ref, o_ref):
    mix = _dot(c_ref[...], wc_ref[...]) + _dot(a_ref[...], wa_ref[...])
    o_ref[...] = ALPHA * x_ref[...] + mix


MIX_BN = 1024


def _mixout_call(conv_out, attn_out, w_out, x):
    t = x.shape[0]
    return pl.pallas_call(
        _mixout_body,
        grid=(t // PROJ_BM, D_MODEL // MIX_BN),
        in_specs=[
            pl.BlockSpec((PROJ_BM, CONV_WIDTH), lambda i, j: (i, 0)),
            pl.BlockSpec((PROJ_BM, ATTN_WIDTH), lambda i, j: (i, 0)),
            pl.BlockSpec((CONV_WIDTH, MIX_BN), lambda i, j: (0, j)),
            pl.BlockSpec((ATTN_WIDTH, MIX_BN), lambda i, j: (1, j)),
            pl.BlockSpec((PROJ_BM, MIX_BN), lambda i, j: (i, j)),
        ],
        out_specs=pl.BlockSpec((PROJ_BM, MIX_BN), lambda i, j: (i, j)),
        out_shape=jax.ShapeDtypeStruct((t, D_MODEL), F32),
        compiler_params=_params(("parallel", "arbitrary")),
        name="mixout",
    )(conv_out, attn_out, w_out, w_out, x)


XA_BM = 256


def _xattn_body(z_ref, g1_ref, b1_ref, wq_ref, k_ref, v_ref, wo_ref, g2_ref, b2_ref, o_ref, ob_ref,
                x1_ref, x1b_ref, z2_ref):
    def store_x1(r0, x1):
        x1_ref[pl.ds(r0, LN_ROWS), :] = x1
        x1b_ref[pl.ds(r0, LN_ROWS), :] = x1.astype(BF16)

    _ln_loop(XA_BM, lambda r0: z_ref[pl.ds(r0, LN_ROWS), :], store_x1, g1_ref, b1_ref)

    q = _dot(x1b_ref[...], wq_ref[...])
    outs = []
    for h in range(X_HEADS):
        cs = slice(h * HEAD_DIM, (h + 1) * HEAD_DIM)
        s = _dot_nt(q[:, cs].astype(BF16), k_ref[0, :, cs]) * SCALE
        m = jnp.max(s, axis=-1, keepdims=True)
        e = jnp.exp(s - m)
        p = (e / jnp.sum(e, axis=-1, keepdims=True)).astype(BF16)
        outs.append(_dot(p, v_ref[0, :, cs]))
    o = jnp.concatenate(outs, axis=-1).astype(BF16)
    z2_ref[...] = ALPHA * x1_ref[...] + _dot(o, wo_ref[...])

    def store_x2(r0, x2):
        o_ref[pl.ds(r0, LN_ROWS), :] = x2
        ob_ref[pl.ds(r0, LN_ROWS), :] = x2.astype(BF16)

    _ln_loop(XA_BM, lambda r0: z2_ref[pl.ds(r0, LN_ROWS), :], store_x2, g2_ref, b2_ref)


def _xattn_call(z1, ln1_g, ln1_b, xq_w, kx, vx, xo_w, ln2_g, ln2_b, seq_len):
    t = z1.shape[0]
    per = seq_len // XA_BM
    vec = pl.BlockSpec((1, D_MODEL), lambda i: (0, 0))
    return pl.pallas_call(
        _xattn_body,
        grid=(t // XA_BM,),
        in_specs=[
            pl.BlockSpec((XA_BM, D_MODEL), lambda i: (i, 0)),
            vec, vec,
            pl.BlockSpec((D_MODEL, X_WIDTH), lambda i: (0, 0)),
            pl.BlockSpec((1, MEM_LEN, X_WIDTH), lambda i: (i // per, 0, 0)),
            pl.BlockSpec((1, MEM_LEN, X_WIDTH), lambda i: (i // per, 0, 0)),
            pl.BlockSpec((X_WIDTH, D_MODEL), lambda i: (0, 0)),
            vec, vec,
        ],
        out_specs=[
            pl.BlockSpec((XA_BM, D_MODEL), lambda i: (i, 0)),
            pl.BlockSpec((XA_BM, D_MODEL), lambda i: (i, 0)),
        ],
        out_shape=[
            jax.ShapeDtypeStruct((t, D_MODEL), F32),
            jax.ShapeDtypeStruct((t, D_MODEL), BF16),
        ],
        scratch_shapes=[
            pltpu.VMEM((XA_BM, D_MODEL), F32),
            pltpu.VMEM((XA_BM, D_MODEL), BF16),
            pltpu.VMEM((XA_BM, D_MODEL), F32),
        ],
        compiler_params=_params(("parallel",)),
        name="xattn",
    )(z1, ln1_g.reshape(1, D_MODEL), ln1_b.reshape(1, D_MODEL), xq_w, kx, vx, xo_w,
      ln2_g.reshape(1, D_MODEL), ln2_b.reshape(1, D_MODEL))


FFN_BM = 1024
FFN_TF = 256
FFN_NJ = D_FF // FFN_TF
FFN_EP = FFN_BM // LN_ROWS
FFN_ZERO_ROWS = 64


def _ffn_body(xb_ref, wg_ref, wu_ref, wd_ref, g_ref, b_ref, x_hbm, o_hbm, acc_ref, xres_ref, xsem, osem):
    i = pl.program_id(0)
    j = pl.program_id(1)
    row0 = i * FFN_BM

    def x_copy(c):
        return pltpu.make_async_copy(
            x_hbm.at[pl.ds(row0 + c * LN_ROWS, LN_ROWS), :], xres_ref.at[c % 2], xsem.at[c % 2])

    def o_copy(c):
        return pltpu.make_async_copy(
            acc_ref.at[pl.ds(c * LN_ROWS, LN_ROWS), :],
            o_hbm.at[pl.ds(row0 + c * LN_ROWS, LN_ROWS), :], osem.at[c])

    @pl.when(j == 0)
    def _():
        def rows(r, carry):
            r0 = pl.multiple_of(r * FFN_ZERO_ROWS, FFN_ZERO_ROWS)
            acc_ref[pl.ds(r0, FFN_ZERO_ROWS), :] = jnp.zeros((FFN_ZERO_ROWS, D_MODEL), F32)
            return carry

        lax.fori_loop(0, FFN_BM // FFN_ZERO_ROWS, rows, 0)

    @pl.when(j == FFN_NJ - 1)
    def _():
        x_copy(0).start()
        x_copy(1).start()

    xb = xb_ref[...]
    gate = _dot(xb, wg_ref[...])
    up = _dot(xb, wu_ref[...])
    a = (gate * jax.nn.sigmoid(gate) * up).astype(BF16)
    acc_ref[...] += _dot(a, wd_ref[...])

    @pl.when(j == FFN_NJ - 1)
    def _():
        for c in range(FFN_EP):
            rs = slice(c * LN_ROWS, (c + 1) * LN_ROWS)
            x_copy(c).wait()
            z = ALPHA * xres_ref[c % 2] + acc_ref[rs, :]
            acc_ref[rs, :] = _ln_rows(z, g_ref[...], b_ref[...])
            if c + 2 < FFN_EP:
                x_copy(c + 2).start()
            o_copy(c).start()
        for c in range(FFN_EP):
            o_copy(c).wait()


def _ffn_call(x2, x2b, w_gate, w_up, w_down, ln_g, ln_b):
    t = x2.shape[0]
    return pl.pallas_call(
        _ffn_body,
        grid=(t // FFN_BM, FFN_NJ),
        in_specs=[
            pl.BlockSpec((FFN_BM, D_MODEL), lambda i, j: (i, 0)),
            pl.BlockSpec((D_MODEL, FFN_TF), lambda i, j: (0, j)),
            pl.BlockSpec((D_MODEL, FFN_TF), lambda i, j: (0, j)),
            pl.BlockSpec((FFN_TF, D_MODEL), lambda i, j: (j, 0)),
            pl.BlockSpec((1, D_MODEL), lambda i, j: (0, 0)),
            pl.BlockSpec((1, D_MODEL), lambda i, j: (0, 0)),
            pl.BlockSpec(memory_space=pl.ANY),
        ],
        out_specs=pl.BlockSpec(memory_space=pl.ANY),
        out_shape=jax.ShapeDtypeStruct((t, D_MODEL), F32),
        scratch_shapes=[
            pltpu.VMEM((FFN_BM, D_MODEL), F32),
            pltpu.VMEM((2, LN_ROWS, D_MODEL), F32),
            pltpu.SemaphoreType.DMA((2,)),
            pltpu.SemaphoreType.DMA((FFN_EP,)),
        ],
        compiler_params=_params(("arbitrary", "arbitrary")),
        name="ffn",
    )(x2b, w_gate, w_up, w_down, ln_g.reshape(1, D_MODEL), ln_b.reshape(1, D_MODEL), x2)


def _layer_tail(x, attn_out, u, kx, vx, w, seq_len):
    conv_out = _conv_call(u, w["conv_w"], w["conv_b"], w["conv_ln_g"], w["conv_ln_b"], seq_len)
    z1 = _mixout_call(conv_out, attn_out, w["w_out"], x)
    x2, x2b = _xattn_call(z1, w["ln1_g"], w["ln1_b"], w["xq_w"], kx, vx, w["xo_w"], w["ln2_g"],
                          w["ln2_b"], seq_len)
    return _ffn_call(x2, x2b, w["w_gate"], w["w_up"], w["w_down"], w["ln3_g"], w["ln3_b"])


def kernel(x_prompt, x_sample, mem_prompt, mem_sample, rel_bias, w_in, conv_w, conv_b, conv_ln_g,
           conv_ln_b, sink, w_out, ln1_g, ln1_b, xq_w, xk_w, xv_w, xo_w, ln2_g, ln2_b, w_gate, w_up,
           w_down, ln3_g, ln3_b):
    bp, sp, _ = x_prompt.shape
    bs, ss, _ = x_sample.shape
    xs = [x_prompt.reshape(bp * sp, D_MODEL), x_sample.reshape(bs * ss, D_MODEL)]
    mems = [mem_prompt, mem_sample]
    bias = _bias_call(rel_bias)
    for l in range(DEPTH):
        w_glu_b = w_in[l][:, :2 * CONV_WIDTH].astype(BF16)
        w_qkv_b = w_in[l][:, 2 * CONV_WIDTH:].astype(BF16)
        qkv_p, xb_p = _qkv_call(xs[0], w_qkv_b)
        qkv_s, xb_s = _qkv_call(xs[1], w_qkv_b)
        u_p, w_gate_b = _glu_call(xb_p, w_glu_b, w_gate[l])
        u_s, w_up_b = _glu_call(xb_s, w_glu_b, w_up[l])
        attn_p, w_down_b = _attn_call(qkv_p, bias, sink[l], sp, w_down[l])
        (attn_s,) = _attn_call(qkv_s, bias, sink[l], ss)
        w = {
            "conv_w": conv_w[l], "conv_b": conv_b[l],
            "conv_ln_g": conv_ln_g[l], "conv_ln_b": conv_ln_b[l],
            "w_out": w_out[l].astype(BF16), "ln1_g": ln1_g[l], "ln1_b": ln1_b[l],
            "xq_w": xq_w[l].astype(BF16), "xo_w": xo_w[l].astype(BF16),
            "ln2_g": ln2_g[l], "ln2_b": ln2_b[l],
            "w_gate": w_gate_b, "w_up": w_up_b, "w_down": w_down_b,
            "ln3_g": ln3_g[l], "ln3_b": ln3_b[l],
        }
        wkv = jnp.concatenate([xk_w[l], xv_w[l]], axis=-1).astype(BF16)
        memb = jnp.concatenate([m.reshape(-1, D_MODEL) for m in mems], axis=0).astype(BF16)
        kv = _mm_call(memb, wkv, 0, 2 * X_WIDTH, MEM_LEN, X_WIDTH, "memkv")
        kv = kv.reshape(bp + bs, MEM_LEN, 2 * X_WIDTH)
        kx, vx = kv[..., :X_WIDTH], kv[..., X_WIDTH:]
        xs = [
            _layer_tail(xs[0], attn_p, u_p, kx[:bp], vx[:bp], w, sp),
            _layer_tail(xs[1], attn_s, u_s, kx[bp:], vx[bp:], w, ss),
        ]
    return (xs[0].reshape(bp, sp, D_MODEL), xs[1].reshape(bs, ss, D_MODEL))
```
